```python
import math
import jax, jax.numpy as jnp
from jax import lax
import numpy as np

D_MODEL = 1024
BATCH = 32
SEQ = 2048
DEPTH = 1

GLA_HEADS = 4
GLA_DK = 128
GLA_DV = 256
GLA_GATE_RANK = 16
GLA_TAU = 16.0
GLA_CHUNK = 64

MLA_HEADS = 8
MLA_Q_RANK = 384
MLA_KV_RANK = 256
MLA_NOPE = 128
MLA_ROPE = 64
MLA_V = 128
ROPE_THETA = 10000.0
ATTN_BLOCK = 128

D_FF = 2816
CONV_WIDTH = 3

DN_ALPHA = (2.0 * DEPTH) ** 0.25
DN_BETA = (8.0 * DEPTH) ** -0.25
LN_EPS = 1e-5
RMS_EPS = 1e-6

IN_SPLITS = (
    GLA_HEADS * GLA_DK,
    GLA_HEADS * GLA_DK,
    GLA_HEADS * GLA_DV,
    GLA_GATE_RANK,
    GLA_HEADS * GLA_DV,
    MLA_Q_RANK,
    MLA_KV_RANK,
    MLA_ROPE,
    D_MODEL,
    D_MODEL,
)
D_IN = sum(IN_SPLITS)

kernel_name = "hybrid_gla_mla_convffn_deepnorm"


def _rmsnorm(x, g):
    xf = x.astype(jnp.float32)
    y = xf * lax.rsqrt(jnp.mean(xf * xf, axis=-1, keepdims=True) + RMS_EPS)
    return (y * g.astype(jnp.float32)).astype(x.dtype)


def _layernorm(x, g, b):
    xf = x.astype(jnp.float32)
    mu = jnp.mean(xf, axis=-1, keepdims=True)
    xc = xf - mu
    var = jnp.mean(xc * xc, axis=-1, keepdims=True)
    y = xc * lax.rsqrt(var + LN_EPS) * g.astype(jnp.float32) + b.astype(jnp.float32)
    return y.astype(x.dtype)


def _rope_tables(positions):
    half = MLA_ROPE // 2
    inv_freq = ROPE_THETA ** (-jnp.arange(half, dtype=jnp.float32) / half)
    ang = positions.astype(jnp.float32)[..., None] * inv_freq
    return jnp.cos(ang), jnp.sin(ang)


def _rope(x, cos, sin):
    half = MLA_ROPE // 2
    xf = x.astype(jnp.float32)
    x1, x2 = xf[..., :half], xf[..., half:]
    out = jnp.concatenate([x1 * cos - x2 * sin, x2 * cos + x1 * sin], axis=-1)
    return out.astype(x.dtype)


def _gla_chunked(q, k, v, log_a):
    B, S, H, DK = q.shape
    DV = v.shape[-1]
    C = GLA_CHUNK
    N = S // C

    def chunk(t):
        return t.astype(jnp.float32).reshape(B, N, C, H, t.shape[-1]).transpose(1, 0, 3, 2, 4)

    qc = chunk(q) * (DK ** -0.5)
    kc = chunk(k)
    vc = chunk(v)
    b = jnp.cumsum(chunk(log_a), axis=3)
    b_last = b[:, :, :, -1:, :]
    q_in = qc * jnp.exp(b)
    k_in = kc * jnp.exp(-b)
    k_st = kc * jnp.exp(b_last - b)
    decay = jnp.exp(b_last[:, :, :, 0, :])

    causal = jnp.tril(jnp.ones((C, C), dtype=bool))
    att = jnp.einsum('nbhid,nbhjd->nbhij', q_in, k_in)
    att = jnp.where(causal, att, 0.0)
    o_intra = jnp.einsum('nbhij,nbhjv->nbhiv', att, vc)

    def step(state, inp):
        q_n, k_n, v_n, d_n = inp
        o_n = jnp.einsum('bhid,bhdv->bhiv', q_n, state)
        state = state * d_n[..., None] + jnp.einsum('bhjd,bhjv->bhdv', k_n, v_n)
        return state, o_n

    s0 = jnp.zeros((B, H, DK, DV), jnp.float32)
    _, o_inter = lax.scan(step, s0, (q_in, k_st, vc, decay))
    o = o_intra + o_inter
    return o.transpose(1, 0, 3, 2, 4).reshape(B, S, H, DV)


def _mla_attention(q_nope, q_rope, k_nope, k_rope, v):
    B, S, H, _ = q_nope.shape
    nblk = S // ATTN_BLOCK
    scale = (MLA_NOPE + MLA_ROPE) ** -0.5
    key_pos = jnp.arange(S)

    def blocks(t):
        return t.reshape((B, nblk, ATTN_BLOCK) + t.shape[2:]).swapaxes(0, 1)

    def attend(args):
        qn, qr, blk = args
        s = (jnp.einsum('bqhd,bkhd->bhqk', qn, k_nope)
             + jnp.einsum('bqhr,bkr->bhqk', qr, k_rope)).astype(jnp.float32) * scale
        q_pos = blk * ATTN_BLOCK + jnp.arange(ATTN_BLOCK)
        mask = key_pos[None, :] <= q_pos[:, None]
        s = jnp.where(mask, s, -jnp.inf)
        p = jax.nn.softmax(s, axis=-1).astype(v.dtype)
        return jnp.einsum('bhqk,bkhv->bqhv', p, v)

    out = lax.map(attend, (blocks(q_nope), blocks(q_rope), jnp.arange(nblk)))
    return out.swapaxes(0, 1).reshape(B, S, H * MLA_V)


def setup_inputs(seed: int = 0) -> dict:
    key = jax.random.key(seed)
    ks = jax.random.split(key, 24)
    f32 = jnp.float32
    L = DEPTH

    def w(k, shape, fan_in, scale=1.0):
        return jax.random.normal(k, shape, f32) * (scale * fan_in ** -0.5)

    def gain(k, shape):
        return 1.0 + 0.02 * jax.random.normal(k, shape, f32)

    def bias(k, shape):
        return 0.02 * jax.random.normal(k, shape, f32)

    x = jax.random.normal(ks[0], (BATCH, SEQ, D_MODEL), f32)
    offs = jax.random.randint(ks[1], (BATCH, 1), 0, 4096, dtype=jnp.int32)
    positions = (offs + jnp.arange(SEQ, dtype=jnp.int32)[None, :]).astype(jnp.int32)
    return {
        "x": x,
        "positions": positions,
        "w_in": w(ks[2], (L, D_MODEL, D_IN), D_MODEL),
        "gla_w_gate_up": w(ks[3], (L, GLA_GATE_RANK, GLA_HEADS * GLA_DK), GLA_GATE_RANK),
        "gla_b_gate": bias(ks[4], (L, GLA_HEADS * GLA_DK)),
        "gla_norm_g": gain(ks[5], (L, GLA_DV)),
        "w_gla_o": w(ks[6], (L, GLA_HEADS * GLA_DV, D_MODEL), GLA_HEADS * GLA_DV),
        "mla_q_norm_g": gain(ks[7], (L, MLA_Q_RANK)),
        "mla_w_uq": w(ks[8], (L, MLA_Q_RANK, MLA_HEADS * (MLA_NOPE + MLA_ROPE)), MLA_Q_RANK),
        "mla_kv_norm_g": gain(ks[9], (L, MLA_KV_RANK)),
        "mla_w_ukv": w(ks[10], (L, MLA_KV_RANK, MLA_HEADS * (MLA_NOPE + MLA_V)), MLA_KV_RANK),
        "w_mla_o": w(ks[11], (L, MLA_HEADS * MLA_V, D_MODEL), MLA_HEADS * MLA_V),
        "w_out": w(ks[12], (L, D_MODEL, D_MODEL), D_MODEL, DN_BETA),
        "ln1_g": gain(ks[13], (L, D_MODEL)),
        "ln1_b": bias(ks[14], (L, D_MODEL)),
        "w_up": w(ks[15], (L, D_MODEL, 2 * D_FF), D_MODEL),
        "conv_w": w(ks[16], (L, CONV_WIDTH, 2 * D_FF), CONV_WIDTH),
        "conv_b": bias(ks[17], (L, 2 * D_FF)),
        "w_down": w(ks[18], (L, D_FF, D_MODEL), D_FF, DN_BETA),
        "ln2_g": gain(ks[19], (L, D_MODEL)),
        "ln2_b": bias(ks[20], (L, D_MODEL)),
    }


def reference(x, positions, w_in, gla_w_gate_up, gla_b_gate, gla_norm_g, w_gla_o,
              mla_q_norm_g, mla_w_uq, mla_kv_norm_g, mla_w_ukv, w_mla_o, w_out,
              ln1_g, ln1_b, w_up, conv_w, conv_b, w_down, ln2_g, ln2_b):
    B, S, D = x.shape
    offsets = []
    acc = 0
    for n in IN_SPLITS[:-1]:
        acc += n
        offsets.append(acc)
    cos, sin = _rope_tables(positions)
    cos_h, sin_h = cos[:, :, None, :], sin[:, :, None, :]

    h = x
    for l in range(DEPTH):
        proj = h @ w_in[l]
        (g_q, g_k, g_v, g_r, g_og, m_cq, m_ckv, m_kr, gate_a, gate_b) = jnp.split(proj, offsets, axis=-1)

        gate_logit = (g_r @ gla_w_gate_up[l] + gla_b_gate[l]).astype(jnp.float32)
        log_a = jax.nn.log_sigmoid(gate_logit) / GLA_TAU
        o = _gla_chunked(g_q.reshape(B, S, GLA_HEADS, GLA_DK),
                         g_k.reshape(B, S, GLA_HEADS, GLA_DK),
                         g_v.reshape(B, S, GLA_HEADS, GLA_DV),
                         log_a.reshape(B, S, GLA_HEADS, GLA_DK)).astype(h.dtype)
        o = _rmsnorm(o, gla_norm_g[l]).reshape(B, S, GLA_HEADS * GLA_DV)
        y_gla = (o * jax.nn.silu(g_og)) @ w_gla_o[l]

        c_q = _rmsnorm(m_cq, mla_q_norm_g[l])
        q = (c_q @ mla_w_uq[l]).reshape(B, S, MLA_HEADS, MLA_NOPE + MLA_ROPE)
        q_nope = q[..., :MLA_NOPE]
        q_rope = _rope(q[..., MLA_NOPE:], cos_h, sin_h)
        c_kv = _rmsnorm(m_ckv, mla_kv_norm_g[l])
        kv = (c_kv @ mla_w_ukv[l]).reshape(B, S, MLA_HEADS, MLA_NOPE + MLA_V)
        k_nope = kv[..., :MLA_NOPE]
        v = kv[..., MLA_NOPE:]
        k_rope = _rope(m_kr, cos, sin)
        y_mla = _mla_attention(q_nope, q_rope, k_nope, k_rope, v) @ w_mla_o[l]

        mixed = (jax.nn.sigmoid(gate_a) * y_gla + jax.nn.sigmoid(gate_b) * y_mla) @ w_out[l]
        h = _layernorm(DN_ALPHA * h + mixed, ln1_g[l], ln1_b[l])

        u = h @ w_up[l]
        u = lax.conv_general_dilated(
            u, conv_w[l][:, None, :].astype(u.dtype), window_strides=(1,),
            padding=[(CONV_WIDTH - 1, 0)],
            dimension_numbers=('NWC', 'WIO', 'NWC'),
            feature_group_count=2 * D_FF) + conv_b[l]
        u_gate, u_val = u[..., :D_FF], u[..., D_FF:]
        f = (jax.nn.silu(u_gate) * u_val) @ w_down[l]
        h = _layernorm(DN_ALPHA * h + f, ln2_g[l], ln2_b[l])
    return h
```

```python
import functools

import jax
import jax.numpy as jnp
from jax import lax
from jax.experimental import pallas as pl
from jax.experimental.pallas import tpu as pltpu

F32 = jnp.float32
BF16 = jnp.bfloat16

D_MODEL = 1024
GLA_HEADS = 4
GLA_DK = 128
GLA_DV = 256
GLA_RANK = 16
GLA_TAU = 16.0
GLA_CHUNK = 64
MLA_HEADS = 8
MLA_Q_RANK = 384
MLA_KV_RANK = 256
MLA_NOPE = 128
MLA_ROPE = 64
MLA_V = 128
ROPE_THETA = 10000.0
D_FF = 2816
DN_ALPHA = 2.0 ** 0.25
LN_EPS = 1e-5
RMS_EPS = 1e-6

LANES = 128
MLA_QK_PAD = 256
NEG_BIG = -1e30

C_Q, C_K, C_V = 0, 512, 1024
C_CQ = 2048
C_CKV = C_CQ + MLA_Q_RANK
C_KR = C_CKV + MLA_KV_RANK
C_R = C_KR + LANES
C_END = C_R + LANES

VMEM_LIMIT = 56 * 1024 * 1024


def _sigmoid(x):
    return 1.0 / (1.0 + jnp.exp(-x))


def _dot(a, b):
    return jnp.dot(a, b, preferred_element_type=F32)


def _dot_nt(a, b):
    return lax.dot_general(a, b, (((1,), (1,)), ((), ())), preferred_element_type=F32)


def _rmsnorm(x, g):
    ms = jnp.mean(x * x, axis=-1, keepdims=True)
    return x * lax.rsqrt(ms + RMS_EPS) * g


def _layernorm(x, g, b):
    mu = jnp.mean(x, axis=-1, keepdims=True)
    xc = x - mu
    var = jnp.mean(xc * xc, axis=-1, keepdims=True)
    return xc * lax.rsqrt(var + LN_EPS) * g + b


def _const_spec(shape):
    nd = len(shape)
    return pl.BlockSpec(shape, lambda *_: (0,) * nd)


def _inproj_kernel(x_ref, pos_ref, invf_ref, w_ref, wuq_ref, wukv_ref, wgu_ref, bg_ref,
                   gqn_ref, gkvn_ref,
                   gq_ref, gk_ref, gv_ref, la_ref, mq_ref, mk_ref, mv_ref):
    xb = x_ref[...].astype(BF16)

    def proj(lo, hi):
        return _dot(xb, w_ref[:, lo:hi])

    gq_ref[...] = proj(C_Q, C_K) * (GLA_DK ** -0.5)
    gk_ref[...] = proj(C_K, C_V)
    gv_ref[...] = proj(C_V, C_CQ).astype(BF16)

    r = proj(C_R, C_END).astype(BF16)
    logit = _dot(r, wgu_ref[...]) + bg_ref[...]
    log_sig = jnp.minimum(logit, 0.0) - jnp.log(1.0 + jnp.exp(-jnp.abs(logit)))
    la_ref[...] = log_sig / GLA_TAU

    ang = pos_ref[...] * invf_ref[...]
    lane = lax.broadcasted_iota(jnp.int32, ang.shape, 1)
    cos = jnp.cos(ang)
    sin = jnp.sin(ang)
    cos_a = jnp.where(lane < 2 * (MLA_ROPE // 2), cos, 0.0)
    sin_a = jnp.where(lane < MLA_ROPE // 2, -sin, jnp.where(lane < MLA_ROPE, sin, 0.0))

    def rope(blk):
        return blk * cos_a + pltpu.roll(blk, MLA_ROPE // 2, 1) * sin_a

    scale = (MLA_NOPE + MLA_ROPE) ** -0.5
    cq = _rmsnorm(proj(C_CQ, C_CKV), gqn_ref[...]).astype(BF16)
    q = _dot(cq, wuq_ref[...])
    nope_w = MLA_HEADS * MLA_NOPE
    for h in range(MLA_HEADS):
        mq_ref[0, h, :, 0:LANES] = (q[:, h * LANES:(h + 1) * LANES] * scale).astype(BF16)
        blk = q[:, nope_w + h * LANES: nope_w + (h + 1) * LANES]
        mq_ref[0, h, :, LANES:2 * LANES] = (rope(blk) * scale).astype(BF16)

    ckv = _rmsnorm(proj(C_CKV, C_KR), gkvn_ref[...]).astype(BF16)
    kv = _dot(ckv, wukv_ref[...])
    kr = rope(proj(C_KR, C_R)).astype(BF16)
    for h in range(MLA_HEADS):
        base = h * (MLA_NOPE + MLA_V)
        mk_ref[0, h, :, 0:LANES] = kv[:, base:base + MLA_NOPE].astype(BF16)
        mk_ref[0, h, :, LANES:2 * LANES] = kr
        mv_ref[0, h, :, :] = kv[:, base + MLA_NOPE:base + MLA_NOPE + MLA_V].astype(BF16)


def _inproj(x2, pos_b, invf, w_all, wuq, wukv, wgu, bg, gqn, gkvn, batch, seq, tm):
    t = x2.shape[0]
    nj = seq // tm
    grid = (t // tm,)
    row = lambda w: pl.BlockSpec((tm, w), lambda i: (i, 0))
    head = lambda w: pl.BlockSpec((1, MLA_HEADS, tm, w), lambda i: (i // nj, 0, i % nj, 0))
    out_shape = (
        jax.ShapeDtypeStruct((t, GLA_HEADS * GLA_DK), F32),
        jax.ShapeDtypeStruct((t, GLA_HEADS * GLA_DK), F32),
        jax.ShapeDtypeStruct((t, GLA_HEADS * GLA_DV), BF16),
        jax.ShapeDtypeStruct((t, GLA_HEADS * GLA_DK), F32),
        jax.ShapeDtypeStruct((batch, MLA_HEADS, seq, MLA_QK_PAD), BF16),
        jax.ShapeDtypeStruct((batch, MLA_HEADS, seq, MLA_QK_PAD), BF16),
        jax.ShapeDtypeStruct((batch, MLA_HEADS, seq, MLA_V), BF16),
    )
    return pl.pallas_call(
        _inproj_kernel,
        grid=grid,
        in_specs=[row(D_MODEL), row(LANES), _const_spec(invf.shape), _const_spec(w_all.shape),
                  _const_spec(wuq.shape), _const_spec(wukv.shape), _const_spec(wgu.shape),
                  _const_spec(bg.shape), _const_spec(gqn.shape), _const_spec(gkvn.shape)],
        out_specs=(row(512), row(512), row(1024), row(512),
                   head(MLA_QK_PAD), head(MLA_QK_PAD), head(MLA_V)),
        out_shape=out_shape,
        compiler_params=pltpu.CompilerParams(
            dimension_semantics=("arbitrary",), vmem_limit_bytes=VMEM_LIMIT),
        name="inproj",
    )(x2, pos_b, invf, w_all, wuq, wukv, wgu, bg, gqn, gkvn)


def _split3_dot(tri, x):
    hi = x.astype(BF16)
    r1 = x - hi.astype(F32)
    mid = r1.astype(BF16)
    lo = (r1 - mid.astype(F32)).astype(BF16)
    return _dot(tri, hi) + _dot(tri, mid) + _dot(tri, lo)


def _gla_kernel(q_ref, k_ref, v_ref, la_ref, g_ref, tri_ref, o_ref, st_ref, *, rows):
    @pl.when(pl.program_id(2) == 0)
    def _():
        st_ref[...] = jnp.zeros_like(st_ref)

    c = GLA_CHUNK
    b_all = _split3_dot(tri_ref[...], la_ref[...])
    row_i = lax.broadcasted_iota(jnp.int32, (c, c), 0)
    col_i = lax.broadcasted_iota(jnp.int32, (c, c), 1)
    causal = col_i <= row_i
    g = g_ref[...]
    for n in range(rows // c):
        sl = slice(n * c, (n + 1) * c)
        b = b_all[sl, :]
        b_last = b[c - 1:c, :]
        q = q_ref[sl, :]
        k = k_ref[sl, :]
        v = v_ref[sl, :]
        q_in = (q * jnp.exp(b)).astype(BF16)
        k_in = (k * jnp.exp(-b)).astype(BF16)
        k_st = k * jnp.exp(b_last - b)
        decay = jnp.exp(b_last)
        att = jnp.where(causal, _dot_nt(q_in, k_in), 0.0).astype(BF16)
        state = st_ref[...]
        o = _dot(att, v) + _dot_nt(q_in, state.astype(BF16))
        st_ref[...] = state * decay + _dot(v.astype(F32).T.astype(BF16), k_st.astype(BF16))
        o_ref[sl, :] = _rmsnorm(o, g)


def _gla(gq, gk, gv, la, g, tri, batch, seq, rows):
    t = gq.shape[0]
    nr = seq // rows
    grid = (batch, GLA_HEADS, nr)
    blk = lambda w: pl.BlockSpec((rows, w), lambda b, h, r: (b * nr + r, h))
    return pl.pallas_call(
        functools.partial(_gla_kernel, rows=rows),
        grid=grid,
        in_specs=[blk(GLA_DK), blk(GLA_DK), blk(GLA_DV), blk(GLA_DK),
                  _const_spec(g.shape), _const_spec(tri.shape)],
        out_specs=blk(GLA_DV),
        out_shape=jax.ShapeDtypeStruct((t, GLA_HEADS * GLA_DV), F32),
        scratch_shapes=[pltpu.VMEM((GLA_DV, GLA_DK), F32)],
        compiler_params=pltpu.CompilerParams(
            dimension_semantics=("arbitrary", "arbitrary", "arbitrary"),
            vmem_limit_bytes=VMEM_LIMIT),
        name="gla",
    )(gq, gk, gv, la, g, tri)


def _mla_kernel(q_ref, k_ref, v_ref, o_ref, *, tq):
    qi = pl.program_id(2)
    q = q_ref[0, 0]

    def block(kb, carry, masked):
        m, l, acc = carry
        start = pl.multiple_of(kb * tq, tq)
        k = k_ref[0, 0, pl.ds(start, tq), :]
        v = v_ref[0, 0, pl.ds(start, tq), :]
        s = _dot_nt(q, k)
        if masked:
            row_i = lax.broadcasted_iota(jnp.int32, s.shape, 0)
            col_i = lax.broadcasted_iota(jnp.int32, s.shape, 1)
            s = jnp.where(col_i <= row_i, s, NEG_BIG)
        m_new = jnp.maximum(m, jnp.max(s, axis=-1, keepdims=True))
        p = jnp.exp(s - m_new)
        alpha = jnp.exp(m - m_new)
        l = alpha * l + jnp.sum(p, axis=-1, keepdims=True)
        acc = alpha * acc + _dot(p.astype(BF16), v)
        return m_new, l, acc

    init = (jnp.full((tq, 1), NEG_BIG, F32), jnp.zeros((tq, 1), F32), jnp.zeros((tq, MLA_V), F32))
    carry = lax.fori_loop(0, qi, lambda kb, cr: block(kb, cr, False), init)
    _, l, acc = block(qi, carry, True)
    o_ref[...] = (acc / l).astype(BF16)


def _mla(mq, mk, mv, batch, seq, tq):
    nq = seq // tq
    grid = (batch, MLA_HEADS, nq)
    return pl.pallas_call(
        functools.partial(_mla_kernel, tq=tq),
        grid=grid,
        in_specs=[pl.BlockSpec((1, 1, tq, MLA_QK_PAD), lambda b, h, i: (b, h, i, 0)),
                  pl.BlockSpec((1, 1, seq, MLA_QK_PAD), lambda b, h, i: (b, h, 0, 0)),
                  pl.BlockSpec((1, 1, seq, MLA_V), lambda b, h, i: (b, h, 0, 0))],
        out_specs=pl.BlockSpec((tq, MLA_V), lambda b, h, i: (b * nq + i, h)),
        out_shape=jax.ShapeDtypeStruct((batch * seq, MLA_HEADS * MLA_V), BF16),
        compiler_params=pltpu.CompilerParams(
            dimension_semantics=("arbitrary", "arbitrary", "arbitrary"),
            vmem_limit_bytes=VMEM_LIMIT),
        name="mla",
    )(mq, mk, mv)


def _merge_kernel(x_ref, on_ref, om_ref, wg_ref, wgo_ref, wmo_ref, wout_ref, g_ref, b_ref, h_ref):
    x = x_ref[...]
    xb = x.astype(BF16)
    og = _dot(xb, wg_ref[:, 0:D_MODEL])
    gated = (on_ref[...] * (og * _sigmoid(og))).astype(BF16)
    y_gla = _dot(gated, wgo_ref[...])
    y_mla = _dot(om_ref[...], wmo_ref[...])
    ga = _sigmoid(_dot(xb, wg_ref[:, D_MODEL:2 * D_MODEL]))
    gb = _sigmoid(_dot(xb, wg_ref[:, 2 * D_MODEL:3 * D_MODEL]))
    mixed = _dot((ga * y_gla + gb * y_mla).astype(BF16), wout_ref[...])
    h_ref[...] = _layernorm(DN_ALPHA * x + mixed, g_ref[...], b_ref[...])


def _merge(x2, o_norm, o_mla, wg, wgo, wmo, wout, g, b, tm):
    t = x2.shape[0]
    row = pl.BlockSpec((tm, D_MODEL), lambda i: (i, 0))
    return pl.pallas_call(
        _merge_kernel,
        grid=(t // tm,),
        in_specs=[row, row, row, _const_spec(wg.shape), _const_spec(wgo.shape),
                  _const_spec(wmo.shape), _const_spec(wout.shape),
                  _const_spec(g.shape), _const_spec(b.shape)],
        out_specs=row,
        out_shape=jax.ShapeDtypeStruct((t, D_MODEL), F32),
        compiler_params=pltpu.CompilerParams(
            dimension_semantics=("arbitrary",), vmem_limit_bytes=VMEM_LIMIT),
        name="merge",
    )(x2, o_norm, o_mla, wg, wgo, wmo, wout, g, b)


FFN_CHUNK = 256
FFN_NCHUNK = D_FF // FFN_CHUNK
HALO = 16


def _ffn_kernel(h_ref, halo_ref, wug_ref, wuv_ref, cwg_ref, cwv_ref, cbg_ref, cbv_ref, wd_ref,
                g_ref, b_ref, o_ref, xs_ref, acc_ref, *, tm, tiles_per_seq):
    i = pl.program_id(0)
    h = h_ref[...]
    seq_start = (i % tiles_per_seq) == 0
    xs_ref[0:HALO, :] = jnp.where(seq_start, 0.0, halo_ref[...]).astype(BF16)
    xs_ref[HALO:HALO + tm, :] = h.astype(BF16)
    acc_ref[...] = jnp.zeros_like(acc_ref)

    def conv(u, cw, cb):
        y = cw[2:3, :] * u + cw[1:2, :] * pltpu.roll(u, 1, 0) + cw[0:1, :] * pltpu.roll(u, 2, 0) + cb
        return y[HALO:, :]

    def body(j, carry):
        xs = xs_ref[...]
        a = conv(_dot(xs, wug_ref[j]), cwg_ref[j], cbg_ref[j])
        v = conv(_dot(xs, wuv_ref[j]), cwv_ref[j], cbv_ref[j])
        f = (a * _sigmoid(a) * v).astype(BF16)
        acc_ref[...] += _dot(f, wd_ref[j])
        return carry

    lax.fori_loop(0, FFN_NCHUNK, body, 0)
    o_ref[...] = _layernorm(DN_ALPHA * h + acc_ref[...], g_ref[...], b_ref[...])


def _ffn(h1, wug, wuv, cwg, cwv, cbg, cbv, wd, g, b, seq, tm):
    t = h1.shape[0]
    row = pl.BlockSpec((tm, D_MODEL), lambda i: (i, 0))
    halo = pl.BlockSpec((HALO, D_MODEL), lambda i: (jnp.maximum(i * (tm // HALO) - 1, 0), 0))
    return pl.pallas_call(
        functools.partial(_ffn_kernel, tm=tm, tiles_per_seq=seq // tm),
        grid=(t // tm,),
        in_specs=[row, halo, _const_spec(wug.shape), _const_spec(wuv.shape),
                  _const_spec(cwg.shape), _const_spec(cwv.shape),
                  _const_spec(cbg.shape), _const_spec(cbv.shape), _const_spec(wd.shape),
                  _const_spec(g.shape), _const_spec(b.shape)],
        out_specs=row,
        out_shape=jax.ShapeDtypeStruct((t, D_MODEL), F32),
        scratch_shapes=[pltpu.VMEM((HALO + tm, D_MODEL), BF16), pltpu.VMEM((tm, D_MODEL), F32)],
        compiler_params=pltpu.CompilerParams(
            dimension_semantics=("arbitrary",), vmem_limit_bytes=VMEM_LIMIT),
        name="ffn",
    )(h1, h1, wug, wuv, cwg, cwv, cbg, cbv, wd, g, b)


def _pack_weights(w_in, gla_w_gate_up, mla_w_uq, mla_w_ukv):
    o = 0
    parts = {}
    for name, n in (("q", 512), ("k", 512), ("v", 1024), ("r", GLA_RANK), ("og", 1024),
                    ("cq", MLA_Q_RANK), ("ckv", MLA_KV_RANK), ("kr", MLA_ROPE),
                    ("ga", D_MODEL), ("gb", D_MODEL)):
        parts[name] = w_in[:, o:o + n]
        o += n
    half = MLA_ROPE // 2
    kr = parts["kr"]
    kr4 = jnp.concatenate([kr, kr], axis=1)
    r_pad = jnp.pad(parts["r"], ((0, 0), (0, LANES - GLA_RANK)))
    w_all = jnp.concatenate([parts["q"], parts["k"], parts["v"], parts["cq"], parts["ckv"],
                             kr4, r_pad], axis=1).astype(BF16)
    w_gates = jnp.concatenate([parts["og"], parts["ga"], parts["gb"]], axis=1).astype(BF16)

    wuq = mla_w_uq.reshape(MLA_Q_RANK, MLA_HEADS, MLA_NOPE + MLA_ROPE)
    nope = wuq[:, :, :MLA_NOPE].reshape(MLA_Q_RANK, MLA_HEADS * MLA_NOPE)
    rp = wuq[:, :, MLA_NOPE:]
    rope4 = jnp.concatenate([rp, rp], axis=2).reshape(MLA_Q_RANK, MLA_HEADS * LANES)
    wuq_p = jnp.concatenate([nope, rope4], axis=1).astype(BF16)
    wgu = jnp.pad(gla_w_gate_up, ((0, LANES - GLA_RANK), (0, 0))).astype(BF16)
    del half
    return w_all, w_gates, wuq_p, mla_w_ukv.astype(BF16), wgu


def kernel(x, positions, w_in, gla_w_gate_up, gla_b_gate, gla_norm_g, w_gla_o, mla_q_norm_g, mla_w_uq,
           mla_kv_norm_g, mla_w_ukv, w_mla_o, w_out, ln1_g, ln1_b, w_up, conv_w, conv_b, w_down,
           ln2_g, ln2_b):
    batch, seq, d = x.shape
    t = batch * seq
    depth = w_in.shape[0]
    half = MLA_ROPE // 2
    inv_freq = ROPE_THETA ** (-jnp.arange(half, dtype=F32) / half)
    invf = jnp.tile(inv_freq, LANES // half).reshape(1, LANES)
    pos_b = jnp.broadcast_to(positions.astype(F32).reshape(t, 1), (t, LANES))
    rows = 256
    blk = jnp.arange(rows) // GLA_CHUNK
    tri = ((blk[:, None] == blk[None, :]) &
           (jnp.arange(rows)[None, :] <= jnp.arange(rows)[:, None])).astype(BF16)

    h = x.reshape(t, d)
    for l in range(depth):
        w_all, w_gates, wuq, wukv, wgu = _pack_weights(w_in[l], gla_w_gate_up[l], mla_w_uq[l], mla_w_ukv[l])
        gq, gk, gv, la, mq, mk, mv = _inproj(
            h, pos_b, invf, w_all, wuq, wukv, wgu, gla_b_gate[l].reshape(1, -1),
            mla_q_norm_g[l].reshape(1, -1), mla_kv_norm_g[l].reshape(1, -1), batch, seq, tm=256)
        o_norm = _gla(gq, gk, gv, la, gla_norm_g[l].reshape(1, -1), tri, batch, seq, rows)
        o_mla = _mla(mq, mk, mv, batch, seq, tq=256)
        h1 = _merge(h, o_norm, o_mla, w_gates, w_gla_o[l].astype(BF16), w_mla_o[l].astype(BF16),
                    w_out[l].astype(BF16), ln1_g[l].reshape(1, -1), ln1_b[l].reshape(1, -1), tm=512)

        wu = w_up[l].astype(BF16)
        to_chunks = lambda a: a.reshape(a.shape[0], FFN_NCHUNK, FFN_CHUNK).transpose(1, 0, 2)
        wug, wuv = to_chunks(wu[:, :D_FF]), to_chunks(wu[:, D_FF:])
        cwg, cwv = to_chunks(conv_w[l][:, :D_FF]), to_chunks(conv_w[l][:, D_FF:])
        cb = conv_b[l].reshape(1, -1)
        cbg, cbv = to_chunks(cb[:, :D_FF]), to_chunks(cb[:, D_FF:])
        wd = w_down[l].astype(BF16).reshape(FFN_NCHUNK, FFN_CHUNK, D_MODEL)
        h = _ffn(h1, wug, wuv, cwg, cwv, cbg, cbv, wd, ln2_g[l].reshape(1, -1),
                 ln2_b[l].reshape(1, -1), seq, tm=512)
    return h.reshape(batch, seq, d)
```

```python
import functools

import jax
import jax.numpy as jnp
from jax import lax
from jax.experimental import pallas as pl
from jax.experimental.pallas import tpu as pltpu

F32 = jnp.float32
BF16 = jnp.bfloat16

D_MODEL = 1024
GLA_HEADS = 4
GLA_DK = 128
GLA_DV = 256
GLA_RANK = 16
GLA_TAU = 16.0
GLA_CHUNK = 64
MLA_HEADS = 8
MLA_Q_RANK = 384
MLA_KV_RANK = 256
MLA_NOPE = 128
MLA_ROPE = 64
MLA_V = 128
ROPE_THETA = 10000.0
D_FF = 2816
DN_ALPHA = 2.0 ** 0.25
LN_EPS = 1e-5
RMS_EPS = 1e-6

LANES = 128
MLA_QK_PAD = 256
MLA_TK = 512
NEG_BIG = -1e30
LOG2_E = 1.4426950408889634

C_Q, C_K, C_V = 0, 512, 1024
C_CQ = 2048
C_CKV = C_CQ + MLA_Q_RANK
C_KR = C_CKV + MLA_KV_RANK
C_R = C_KR + LANES
C_END = C_R + LANES

VMEM_LIMIT = 56 * 1024 * 1024


def _sigmoid(x):
    return 1.0 / (1.0 + jnp.exp(-x))


def _dot(a, b):
    return jnp.dot(a, b, preferred_element_type=F32)


def _dot_nt(a, b):
    return lax.dot_general(a, b, (((1,), (1,)), ((), ())), preferred_element_type=F32)


def _rmsnorm(x, g):
    ms = jnp.mean(x * x, axis=-1, keepdims=True)
    return x * lax.rsqrt(ms + RMS_EPS) * g


def _layernorm(x, g, b):
    mu = jnp.mean(x, axis=-1, keepdims=True)
    xc = x - mu
    var = jnp.mean(xc * xc, axis=-1, keepdims=True)
    return xc * lax.rsqrt(var + LN_EPS) * g + b


def _const_spec(shape):
    nd = len(shape)
    return pl.BlockSpec(shape, lambda *_: (0,) * nd)


def _inproj_kernel(x_ref, pos_ref, invf_ref, w_ref, wuq_ref, wukv_ref, wgu_ref, bg_ref,
                   gqn_ref, gkvn_ref,
                   gq_ref, gk_ref, gv_ref, la_ref, mq_ref, mk_ref, mvt_ref):
    xb = x_ref[...].astype(BF16)

    def proj(lo, hi):
        return _dot(xb, w_ref[:, lo:hi])

    gq_ref[...] = proj(C_Q, C_K) * (GLA_DK ** -0.5)
    gk_ref[...] = proj(C_K, C_V)
    gv_ref[...] = proj(C_V, C_CQ).astype(BF16)

    r = proj(C_R, C_END).astype(BF16)
    logit = _dot(r, wgu_ref[...]) + bg_ref[...]
    log_sig = jnp.minimum(logit, 0.0) - jnp.log(1.0 + jnp.exp(-jnp.abs(logit)))
    la_ref[...] = log_sig / GLA_TAU

    ang = pos_ref[...] * invf_ref[...]
    lane = lax.broadcasted_iota(jnp.int32, ang.shape, 1)
    cos = jnp.cos(ang)
    sin = jnp.sin(ang)
    cos_a = jnp.where(lane < 2 * (MLA_ROPE // 2), cos, 0.0)
    sin_a = jnp.where(lane < MLA_ROPE // 2, -sin, jnp.where(lane < MLA_ROPE, sin, 0.0))

    def rope(blk):
        return blk * cos_a + pltpu.roll(blk, MLA_ROPE // 2, 1) * sin_a

    scale = (MLA_NOPE + MLA_ROPE) ** -0.5 * LOG2_E
    cq =_rmsnorm(proj(C_CQ, C_CKV), gqn_ref[...]).astype(BF16)
    q = _dot(cq, wuq_ref[...])
    nope_w = MLA_HEADS * MLA_NOPE
    for h in range(MLA_HEADS):
        mq_ref[0, h, :, 0:LANES] = (q[:, h * LANES:(h + 1) * LANES] * scale).astype(BF16)
        blk = q[:, nope_w + h * LANES: nope_w + (h + 1) * LANES]
        mq_ref[0, h, :, LANES:2 * LANES] = (rope(blk) * scale).astype(BF16)

    ckv = _rmsnorm(proj(C_CKV, C_KR), gkvn_ref[...]).astype(BF16)
    kv = _dot(ckv, wukv_ref[...])
    kr = rope(proj(C_KR, C_R)).astype(BF16)
    for h in range(MLA_HEADS):
        base = h * (MLA_NOPE + MLA_V)
        mk_ref[0, h, :, 0:LANES] = kv[:, base:base + MLA_NOPE].astype(BF16)
        mk_ref[0, h, :, LANES:2 * LANES] = kr
        mvt_ref[0, h, 0, :, :] = kv[:, base + MLA_NOPE:base + MLA_NOPE + MLA_V].T.astype(BF16)


def _inproj(x2, pos_b, invf, w_all, wuq, wukv, wgu, bg, gqn, gkvn, batch, seq, tm):
    t = x2.shape[0]
    nj = seq // tm
    grid = (t // tm,)
    row = lambda w: pl.BlockSpec((tm, w), lambda i: (i, 0))
    head = lambda w: pl.BlockSpec((1, MLA_HEADS, tm, w), lambda i: (i // nj, 0, i % nj, 0))
    per_tk = MLA_TK // tm
    vt_spec = pl.BlockSpec((1, MLA_HEADS, 1, MLA_V, tm),
                           lambda i: (i // nj, 0, (i % nj) // per_tk, 0, (i % nj) % per_tk))
    out_shape = (
        jax.ShapeDtypeStruct((t, GLA_HEADS * GLA_DK), F32),
        jax.ShapeDtypeStruct((t, GLA_HEADS * GLA_DK), F32),
        jax.ShapeDtypeStruct((t, GLA_HEADS * GLA_DV), BF16),
        jax.ShapeDtypeStruct((t, GLA_HEADS * GLA_DK), F32),
        jax.ShapeDtypeStruct((batch, MLA_HEADS, seq, MLA_QK_PAD), BF16),
        jax.ShapeDtypeStruct((batch, MLA_HEADS, seq, MLA_QK_PAD), BF16),
        jax.ShapeDtypeStruct((batch, MLA_HEADS, seq // MLA_TK, MLA_V, MLA_TK), BF16),
    )
    return pl.pallas_call(
        _inproj_kernel,
        grid=grid,
        in_specs=[row(D_MODEL), row(LANES), _const_spec(invf.shape), _const_spec(w_all.shape),
                  _const_spec(wuq.shape), _const_spec(wukv.shape), _const_spec(wgu.shape),
                  _const_spec(bg.shape), _const_spec(gqn.shape), _const_spec(gkvn.shape)],
        out_specs=(row(512), row(512), row(1024), row(512),
                   head(MLA_QK_PAD), head(MLA_QK_PAD), vt_spec),
        out_shape=out_shape,
        compiler_params=pltpu.CompilerParams(
            dimension_semantics=("arbitrary",), vmem_limit_bytes=VMEM_LIMIT),
        name="inproj",
    )(x2, pos_b, invf, w_all, wuq, wukv, wgu, bg, gqn, gkvn)


def _split3_dot(tri, x):
    hi = x.astype(BF16)
    r1 = x - hi.astype(F32)
    mid = r1.astype(BF16)
    lo = (r1 - mid.astype(F32)).astype(BF16)
    return _dot(tri, hi) + _dot(tri, mid) + _dot(tri, lo)


def _gla_kernel(q_ref, k_ref, v_ref, la_ref, g_ref, tri_ref, o_ref, st_ref, *, rows):
    @pl.when(pl.program_id(2) == 0)
    def _():
        st_ref[...] = jnp.zeros_like(st_ref)

    c = GLA_CHUNK
    b_all = _split3_dot(tri_ref[...], la_ref[...])
    row_i = lax.broadcasted_iota(jnp.int32, (c, c), 0)
    col_i = lax.broadcasted_iota(jnp.int32, (c, c), 1)
    causal = col_i <= row_i
    g = g_ref[...]
    for n in range(rows // c):
        sl = slice(n * c, (n + 1) * c)
        b = b_all[sl, :]
        b_last = b[c - 1:c, :]
        q = q_ref[sl, :]
        k = k_ref[sl, :]
        v = v_ref[sl, :]
        q_in = (q * jnp.exp(b)).astype(BF16)
        k_in = (k * jnp.exp(-b)).astype(BF16)
        k_st = k * jnp.exp(b_last - b)
        decay = jnp.exp(b_last)
        att = jnp.where(causal, _dot_nt(q_in, k_in), 0.0).astype(BF16)
        state = st_ref[...]
        o = _dot(att, v) + _dot_nt(q_in, state.astype(BF16))
        st_ref[...] = state * decay + _dot(v.astype(F32).T.astype(BF16), k_st.astype(BF16))
        o_ref[sl, :] = _rmsnorm(o, g)


def _gla(gq, gk, gv, la, g, tri, batch, seq, rows):
    t = gq.shape[0]
    nr = seq // rows
    grid = (batch, GLA_HEADS, nr)
    blk = lambda w: pl.BlockSpec((rows, w), lambda b, h, r: (b * nr + r, h))
    return pl.pallas_call(
        functools.partial(_gla_kernel, rows=rows),
        grid=grid,
        in_specs=[blk(GLA_DK), blk(GLA_DK), blk(GLA_DV), blk(GLA_DK),
                  _const_spec(g.shape), _const_spec(tri.shape)],
        out_specs=blk(GLA_DV),
        out_shape=jax.ShapeDtypeStruct((t, GLA_HEADS * GLA_DV), F32),
        scratch_shapes=[pltpu.VMEM((GLA_DV, GLA_DK), F32)],
        compiler_params=pltpu.CompilerParams(
            dimension_semantics=("arbitrary", "arbitrary", "arbitrary"),
            vmem_limit_bytes=VMEM_LIMIT),
        name="gla",
    )(gq, gk, gv, la, g, tri)


def _mla_kernel(q_ref, k_ref, vt_ref, o_ref, *, heads):
    tk = MLA_TK
    qi = pl.program_id(2)

    def block(kb, carry, masked):
        out = []
        for hh in range(heads):
            m, l, acc = carry[hh]
            start = pl.multiple_of(kb * tk, tk)
            k = k_ref[0, hh, pl.ds(start, tk), :]
            vt = vt_ref[0, hh, kb]
            s = _dot_nt(k, q_ref[0, hh])
            if masked:
                key_i = lax.broadcasted_iota(jnp.int32, s.shape, 0)
                qry_i = lax.broadcasted_iota(jnp.int32, s.shape, 1)
                s = jnp.where(key_i <= qry_i, s, NEG_BIG)
            m_new = jnp.maximum(m, jnp.max(s, axis=0, keepdims=True))
            p = jnp.exp2(s - m_new)
            alpha = jnp.exp2(m - m_new)
            l = alpha * l + jnp.sum(p, axis=0, keepdims=True)
            acc = alpha * acc + _dot(vt, p.astype(BF16))
            out.append((m_new, l, acc))
        return tuple(out)

    one = (jnp.full((1, tk), NEG_BIG, F32), jnp.zeros((1, tk), F32), jnp.zeros((MLA_V, tk), F32))
    carry = lax.fori_loop(0, qi, lambda kb, cr: block(kb, cr, False), (one,) * heads)
    carry = block(qi, carry, True)
    for hh in range(heads):
        _, l, acc = carry[hh]
        o_ref[:, hh * MLA_V:(hh + 1) * MLA_V] = (acc / l).T.astype(BF16)


def _mla(mq, mk, mvt, batch, seq, heads):
    tq = MLA_TK
    nq = seq // tq
    grid = (batch, MLA_HEADS // heads, nq)
    return pl.pallas_call(
        functools.partial(_mla_kernel, heads=heads),
        grid=grid,
        in_specs=[pl.BlockSpec((1, heads, tq, MLA_QK_PAD), lambda b, g, i: (b, g, i, 0)),
                  pl.BlockSpec((1, heads, seq, MLA_QK_PAD), lambda b, g, i: (b, g, 0, 0)),
                  pl.BlockSpec((1, heads, nq, MLA_V, tq), lambda b, g, i: (b, g, 0, 0, 0))],
        out_specs=pl.BlockSpec((tq, heads * MLA_V), lambda b, g, i: (b * nq + i, g)),
        out_shape=jax.ShapeDtypeStruct((batch * seq, MLA_HEADS * MLA_V), BF16),
        compiler_params=pltpu.CompilerParams(
            dimension_semantics=("arbitrary", "arbitrary", "arbitrary"),
            vmem_limit_bytes=VMEM_LIMIT),
        name="mla",
    )(mq, mk, mvt)


def _merge_kernel(x_ref, on_ref, om_ref, wg_ref, wgo_ref, wmo_ref, wout_ref, g_ref, b_ref, h_ref):
    x = x_ref[...]
    xb = x.astype(BF16)
    og = _dot(xb, wg_ref[:, 0:D_MODEL])
    gated = (on_ref[...] * (og * _sigmoid(og))).astype(BF16)
    y_gla = _dot(gated, wgo_ref[...])
    y_mla = _dot(om_ref[...], wmo_ref[...])
    ga = _sigmoid(_dot(xb, wg_ref[:, D_MODEL:2 * D_MODEL]))
    gb = _sigmoid(_dot(xb, wg_ref[:, 2 * D_MODEL:3 * D_MODEL]))
    mixed = _dot((ga * y_gla + gb * y_mla).astype(BF16), wout_ref[...])
    h_ref[...] = _layernorm(DN_ALPHA * x + mixed, g_ref[...], b_ref[...])


def _merge(x2, o_norm, o_mla, wg, wgo, wmo, wout, g, b, tm):
    t = x2.shape[0]
    row = pl.BlockSpec((tm, D_MODEL), lambda i: (i, 0))
    return pl.pallas_call(
        _merge_kernel,
        grid=(t // tm,),
        in_specs=[row, row, row, _const_spec(wg.shape), _const_spec(wgo.shape),
                  _const_spec(wmo.shape), _const_spec(wout.shape),
                  _const_spec(g.shape), _const_spec(b.shape)],
        out_specs=row,
        out_shape=jax.ShapeDtypeStruct((t, D_MODEL), F32),
        compiler_params=pltpu.CompilerParams(
            dimension_semantics=("arbitrary",), vmem_limit_bytes=VMEM_LIMIT),
        name="merge",
    )(x2, o_norm, o_mla, wg, wgo, wmo, wout, g, b)


FFN_CHUNK = 256
FFN_NCHUNK = D_FF // FFN_CHUNK
HALO = 16


def _ffn_kernel(h_ref, halo_ref, wug_ref, wuv_ref, cwg_ref, cwv_ref, cbg_ref, cbv_ref, wd_ref,
                g_ref, b_ref, o_ref, xs_ref, acc_ref, *, tm, tiles_per_seq):
    i = pl.program_id(0)
    h = h_ref[...]
    seq_start = (i % tiles_per_seq) == 0
    xs_ref[0:HALO, :] = jnp.where(seq_start, 0.0, halo_ref[...]).astype(BF16)
    xs_ref[HALO:HALO + tm, :] = h.astype(BF16)
    acc_ref[...] = jnp.zeros_like(acc_ref)

    def conv(u, cw, cb):
        y = cw[2:3, :] * u + cw[1:2, :] * pltpu.roll(u, 1, 0) + cw[0:1, :] * pltpu.roll(u, 2, 0) + cb
        return y[HALO:, :]

    def body(j, carry):
        xs = xs_ref[...]
        a = conv(_dot(xs, wug_ref[j]), cwg_ref[j], cbg_ref[j])
        v = conv(_dot(xs, wuv_ref[j]), cwv_ref[j], cbv_ref[j])
        f = (a * _sigmoid(a) * v).astype(BF16)
        acc_ref[...] += _dot(f, wd_ref[j])
        return carry

    lax.fori_loop(0, FFN_NCHUNK, body, 0)
    o_ref[...] = _layernorm(DN_ALPHA * h + acc_ref[...], g_ref[...], b_ref[...])


def _ffn(h1, wug, wuv, cwg, cwv, cbg, cbv, wd, g, b, seq, tm):
    t = h1.shape[0]
    row = pl.BlockSpec((tm, D_MODEL), lambda i: (i, 0))
    halo = pl.BlockSpec((HALO, D_MODEL), lambda i: (jnp.maximum(i * (tm // HALO) - 1, 0), 0))
    return pl.pallas_call(
        functools.partial(_ffn_kernel, tm=tm, tiles_per_seq=seq // tm),
        grid=(t // tm,),
        in_specs=[row, halo, _const_spec(wug.shape), _const_spec(wuv.shape),
                  _const_spec(cwg.shape), _const_spec(cwv.shape),
                  _const_spec(cbg.shape), _const_spec(cbv.shape), _const_spec(wd.shape),
                  _const_spec(g.shape), _const_spec(b.shape)],
        out_specs=row,
        out_shape=jax.ShapeDtypeStruct((t, D_MODEL), F32),
        scratch_shapes=[pltpu.VMEM((HALO + tm, D_MODEL), BF16), pltpu.VMEM((tm, D_MODEL), F32)],
        compiler_params=pltpu.CompilerParams(
            dimension_semantics=("arbitrary",), vmem_limit_bytes=VMEM_LIMIT),
        name="ffn",
    )(h1, h1, wug, wuv, cwg, cwv, cbg, cbv, wd, g, b)


def _pack_weights(w_in, gla_w_gate_up, mla_w_uq, mla_w_ukv):
    o = 0
    parts = {}
    for name, n in (("q", 512), ("k", 512), ("v", 1024), ("r", GLA_RANK), ("og", 1024),
                    ("cq", MLA_Q_RANK), ("ckv", MLA_KV_RANK), ("kr", MLA_ROPE),
                    ("ga", D_MODEL), ("gb", D_MODEL)):
        parts[name] = w_in[:, o:o + n]
        o += n
    half = MLA_ROPE // 2
    kr = parts["kr"]
    kr4 = jnp.concatenate([kr, kr], axis=1)
    r_pad = jnp.pad(parts["r"], ((0, 0), (0, LANES - GLA_RANK)))
    w_all = jnp.concatenate([parts["q"], parts["k"], parts["v"], parts["cq"], parts["ckv"],
                             kr4, r_pad], axis=1).astype(BF16)
    w_gates = jnp.concatenate([parts["og"], parts["ga"], parts["gb"]], axis=1).astype(BF16)

    wuq = mla_w_uq.reshape(MLA_Q_RANK, MLA_HEADS, MLA_NOPE + MLA_ROPE)
    nope = wuq[:, :, :MLA_NOPE].reshape(MLA_Q_RANK, MLA_HEADS * MLA_NOPE)
    rp = wuq[:, :, MLA_NOPE:]
    rope4 = jnp.concatenate([rp, rp], axis=2).reshape(MLA_Q_RANK, MLA_HEADS * LANES)
    wuq_p = jnp.concatenate([nope, rope4], axis=1).astype(BF16)
    wgu = jnp.pad(gla_w_gate_up, ((0, LANES - GLA_RANK), (0, 0))).astype(BF16)
    del half
    return w_all, w_gates, wuq_p, mla_w_ukv.astype(BF16), wgu


def kernel(x, positions, w_in, gla_w_gate_up, gla_b_gate, gla_norm_g, w_gla_o, mla_q_norm_g, mla_w_uq,
           mla_kv_norm_g, mla_w_ukv, w_mla_o, w_out, ln1_g, ln1_b, w_up, conv_w, conv_b, w_down,
           ln2_g, ln2_b):
    batch, seq, d = x.shape
    t = batch * seq
    depth = w_in.shape[0]
    half = MLA_ROPE // 2
    inv_freq = ROPE_THETA ** (-jnp.arange(half, dtype=F32) / half)
    invf = jnp.tile(inv_freq, LANES // half).reshape(1, LANES)
    pos_b = jnp.broadcast_to(positions.astype(F32).reshape(t, 1), (t, LANES))
    rows = 256
    blk = jnp.arange(rows) // GLA_CHUNK
    tri = ((blk[:, None] == blk[None, :]) &
           (jnp.arange(rows)[None, :] <= jnp.arange(rows)[:, None])).astype(BF16)

    h = x.reshape(t, d)
    for l in range(depth):
        w_all, w_gates, wuq, wukv, wgu = _pack_weights(w_in[l], gla_w_gate_up[l], mla_w_uq[l], mla_w_ukv[l])
        gq, gk, gv, la, mq, mk, mvt = _inproj(
            h, pos_b, invf, w_all, wuq, wukv, wgu, gla_b_gate[l].reshape(1, -1),
            mla_q_norm_g[l].reshape(1, -1), mla_kv_norm_g[l].reshape(1, -1), batch, seq, tm=256)
        o_norm = _gla(gq, gk, gv, la, gla_norm_g[l].reshape(1, -1), tri, batch, seq, rows)
        o_mla = _mla(mq, mk, mvt, batch, seq, heads=2)
        h1 = _merge(h, o_norm, o_mla, w_gates, w_gla_o[l].astype(BF16), w_mla_o[l].astype(BF16),
                    w_out[l].astype(BF16), ln1_g[l].reshape(1, -1), ln1_b[l].reshape(1, -1), tm=512)

        wu = w_up[l].astype(BF16)
        to_chunks = lambda a: a.reshape(a.shape[0], FFN_NCHUNK, FFN_CHUNK).transpose(1, 0, 2)
        wug, wuv = to_chunks(wu[:, :D_FF]), to_chunks(wu[:, D_FF:])
        cwg, cwv = to_chunks(conv_w[l][:, :D_FF]), to_chunks(conv_w[l][:, D_FF:])
        cb = conv_b[l].reshape(1, -1)
        cbg, cbv = to_chunks(cb[:, :D_FF]), to_chunks(cb[:, D_FF:])
        wd = w_down[l].astype(BF16).reshape(FFN_NCHUNK, FFN_CHUNK, D_MODEL)
        h = _ffn(h1, wug, wuv, cwg, cwv, cbg, cbv, wd, ln2_g[l].reshape(1, -1),
                 ln2_b[l].reshape(1, -1), seq, tm=512)
    return h.reshape(batch, seq, d)
```

```python
import functools

import jax
import jax.numpy as jnp
from jax import lax
from jax.experimental import pallas as pl
from jax.experimental.pallas import tpu as pltpu

F32 = jnp.float32
BF16 = jnp.bfloat16

D_MODEL = 1024
GLA_HEADS = 4
GLA_DK = 128
GLA_DV = 256
GLA_RANK = 16
GLA_TAU = 16.0
GLA_CHUNK = 64
MLA_HEADS = 8
MLA_Q_RANK = 384
MLA_KV_RANK = 256
MLA_NOPE = 128
MLA_ROPE = 64
MLA_V = 128
ROPE_THETA = 10000.0
D_FF = 2816
DN_ALPHA = 2.0 ** 0.25
LN_EPS = 1e-5
RMS_EPS = 1e-6

LANES = 128
MLA_QK_PAD = 256
MLA_TILE = 512
MLA_HEADS_PER_STEP = 4
NEG_BIG = -1e30
LOG2_E = 1.4426950408889634

C_Q, C_K, C_V = 0, 512, 1024
C_CQ = 2048
C_CKV = C_CQ + MLA_Q_RANK
C_KR = C_CKV + MLA_KV_RANK
C_R = C_KR + LANES
C_END = C_R + LANES

VMEM_LIMIT = 56 * 1024 * 1024


def _sigmoid(x):
    return 1.0 / (1.0 + jnp.exp(-x))


def _dot(a, b):
    return jnp.dot(a, b, preferred_element_type=F32)


def _dot_nt(a, b):
    return lax.dot_general(a, b, (((1,), (1,)), ((), ())), preferred_element_type=F32)


def _rmsnorm(x, g):
    ms = jnp.mean(x * x, axis=-1, keepdims=True)
    return x * lax.rsqrt(ms + RMS_EPS) * g


def _layernorm(x, g, b):
    mu = jnp.mean(x, axis=-1, keepdims=True)
    xc = x - mu
    var = jnp.mean(xc * xc, axis=-1, keepdims=True)
    return xc * lax.rsqrt(var + LN_EPS) * g + b


def _const_spec(shape):
    nd = len(shape)
    return pl.BlockSpec(shape, lambda *_: (0,) * nd)


def _inproj_kernel(x_ref, pos_ref, invf_ref, w_ref, wuq_ref, wukv_ref, wgu_ref, bg_ref,
                   gqn_ref, gkvn_ref,
                   gq_ref, gk_ref, gv_ref, gvt_ref, la_ref, mq_ref, mk_ref, mvt_ref):
    xb = x_ref[...].astype(BF16)

    def proj(lo, hi):
        return _dot(xb, w_ref[:, lo:hi])

    gq_ref[...] = proj(C_Q, C_K) * (GLA_DK ** -0.5)
    gk_ref[...] = proj(C_K, C_V)
    gv = proj(C_V, C_CQ)
    gv_ref[...] = gv.astype(BF16)
    for h in range(GLA_HEADS):
        gvt_ref[0, h, :, :] = gv[:, h * GLA_DV:(h + 1) * GLA_DV].T.astype(BF16)

    r = proj(C_R, C_END).astype(BF16)
    logit = _dot(r, wgu_ref[...]) + bg_ref[...]
    log_sig = jnp.minimum(logit, 0.0) - jnp.log(1.0 + jnp.exp(-jnp.abs(logit)))
    la_ref[...] = log_sig / GLA_TAU

    ang = pos_ref[...] * invf_ref[...]
    lane = lax.broadcasted_iota(jnp.int32, ang.shape, 1)
    cos = jnp.cos(ang)
    sin = jnp.sin(ang)
    cos_a = jnp.where(lane < MLA_ROPE, cos, 0.0)
    sin_a = jnp.where(lane < MLA_ROPE // 2, -sin, jnp.where(lane < MLA_ROPE, sin, 0.0))

    def rope(blk):
        return blk * cos_a + pltpu.roll(blk, MLA_ROPE // 2, 1) * sin_a

    scale = (MLA_NOPE + MLA_ROPE) ** -0.5 * LOG2_E
    cq = _rmsnorm(proj(C_CQ, C_CKV), gqn_ref[...]).astype(BF16)
    q = _dot(cq, wuq_ref[...])
    nope_w = MLA_HEADS * MLA_NOPE
    for h in range(MLA_HEADS):
        mq_ref[0, h, :, 0:LANES] = (q[:, h * LANES:(h + 1) * LANES] * scale).astype(BF16)
        blk = q[:, nope_w + h * LANES: nope_w + (h + 1) * LANES]
        mq_ref[0, h, :, LANES:2 * LANES] = (rope(blk) * scale).astype(BF16)

    ckv = _rmsnorm(proj(C_CKV, C_KR), gkvn_ref[...]).astype(BF16)
    kv = _dot(ckv, wukv_ref[...])
    kr = rope(proj(C_KR, C_R)).astype(BF16)
    for h in range(MLA_HEADS):
        base = h * (MLA_NOPE + MLA_V)
        mk_ref[0, h, :, 0:LANES] = kv[:, base:base + MLA_NOPE].astype(BF16)
        mk_ref[0, h, :, LANES:2 * LANES] = kr
        mvt_ref[0, h, 0, :, :] = kv[:, base + MLA_NOPE:base + MLA_NOPE + MLA_V].T.astype(BF16)


def _inproj(x2, pos_b, invf, w_all, wuq, wukv, wgu, bg, gqn, gkvn, batch, seq, tm):
    t = x2.shape[0]
    nj = seq // tm
    grid = (t // tm,)
    row = lambda w: pl.BlockSpec((tm, w), lambda i: (i, 0))
    head = lambda w: pl.BlockSpec((1, MLA_HEADS, tm, w), lambda i: (i // nj, 0, i % nj, 0))
    per_tile = MLA_TILE // tm
    vt_spec = pl.BlockSpec((1, MLA_HEADS, 1, MLA_V, tm),
                           lambda i: (i // nj, 0, (i % nj) // per_tile, 0, (i % nj) % per_tile))
    gvt_spec = pl.BlockSpec((1, GLA_HEADS, GLA_DV, tm), lambda i: (i // nj, 0, 0, i % nj))
    out_shape = (
        jax.ShapeDtypeStruct((t, GLA_HEADS * GLA_DK), F32),
        jax.ShapeDtypeStruct((t, GLA_HEADS * GLA_DK), F32),
        jax.ShapeDtypeStruct((t, GLA_HEADS * GLA_DV), BF16),
        jax.ShapeDtypeStruct((batch, GLA_HEADS, GLA_DV, seq), BF16),
        jax.ShapeDtypeStruct((t, GLA_HEADS * GLA_DK), F32),
        jax.ShapeDtypeStruct((batch, MLA_HEADS, seq, MLA_QK_PAD), BF16),
        jax.ShapeDtypeStruct((batch, MLA_HEADS, seq, MLA_QK_PAD), BF16),
        jax.ShapeDtypeStruct((batch, MLA_HEADS, seq // MLA_TILE, MLA_V, MLA_TILE), BF16),
    )
    return pl.pallas_call(
        _inproj_kernel,
        grid=grid,
        in_specs=[row(D_MODEL), row(LANES), _const_spec(invf.shape), _const_spec(w_all.shape),
                  _const_spec(wuq.shape), _const_spec(wukv.shape), _const_spec(wgu.shape),
                  _const_spec(bg.shape), _const_spec(gqn.shape), _const_spec(gkvn.shape)],
        out_specs=(row(512), row(512), row(1024), gvt_spec, row(512),
                   head(MLA_QK_PAD), head(MLA_QK_PAD), vt_spec),
        out_shape=out_shape,
        compiler_params=pltpu.CompilerParams(
            dimension_semantics=("arbitrary",), vmem_limit_bytes=VMEM_LIMIT),
        name="inproj",
    )(x2, pos_b, invf, w_all, wuq, wukv, wgu, bg, gqn, gkvn)


GLA_SUPER = 2 * GLA_CHUNK
GLA_TRI = 2 * GLA_SUPER


def _split2_dot(tri, x):
    hi = x.astype(BF16)
    lo = (x - hi.astype(F32)).astype(BF16)
    return _dot(tri, hi) + _dot(tri, lo)


def _gla_kernel(q_ref, k_ref, v_ref, vt_ref, la_ref, g_ref, tri_ref, o_ref, st_ref, *, rows, heads):
    @pl.when(pl.program_id(2) == 0)
    def _():
        st_ref[...] = jnp.zeros_like(st_ref)

    c = GLA_SUPER
    tri = tri_ref[...]
    b_all = jnp.concatenate([_split2_dot(tri, la_ref[r:r + GLA_TRI, :])
                             for r in range(0, rows, GLA_TRI)], axis=0)
    row_i = lax.broadcasted_iota(jnp.int32, (c, c), 0)
    col_i = lax.broadcasted_iota(jnp.int32, (c, c), 1)
    causal = col_i <= row_i
    g = g_ref[...]
    items = [(n, hh) for n in range(rows // c) for hh in range(heads)]

    def prep(n, hh):
        sl = slice(n * c, (n + 1) * c)
        lanes = slice(hh * GLA_DK, (hh + 1) * GLA_DK)
        b = b_all[sl, lanes]
        b_mid = b[c // 2 - 1:c // 2, :]
        b_end = b[c - 1:c, :]
        q = q_ref[sl, lanes]
        k = k_ref[sl, lanes]
        return dict(
            sl=sl,
            q_mid=(q * jnp.exp(b - b_mid)).astype(BF16),
            k_mid=(k * jnp.exp(b_mid - b)).astype(BF16),
            q_in=(q * jnp.exp(b)).astype(BF16),
            k_end=(k * jnp.exp(b_end - b)).astype(BF16),
            decay=jnp.exp(b_end))

    state = [st_ref[hh] for hh in range(heads)]
    cur = prep(*items[0])
    cur_s = _dot_nt(cur["q_mid"], cur["k_mid"])
    for i, (n, hh) in enumerate(items):
        if i + 1 < len(items):
            nxt = prep(*items[i + 1])
            nxt_s = _dot_nt(nxt["q_mid"], nxt["k_mid"])
        sl = cur["sl"]
        vcol = slice(hh * GLA_DV, (hh + 1) * GLA_DV)
        att = jnp.where(causal, cur_s, 0.0).astype(BF16)
        o = _dot(att, v_ref[sl, vcol]) + _dot_nt(cur["q_in"], state[hh].astype(BF16))
        state[hh] = state[hh] * cur["decay"] + _dot(vt_ref[0, hh, :, sl], cur["k_end"])
        o_ref[sl, vcol] = _rmsnorm(o, g)
        if i + 1 < len(items):
            cur, cur_s = nxt, nxt_s
    for hh in range(heads):
        st_ref[hh] = state[hh]


def _gla(gq, gk, gv, gvt, la, g, tri, batch, seq, rows, heads):
    t = gq.shape[0]
    nr = seq // rows
    grid = (batch, GLA_HEADS // heads, nr)
    blk = lambda w: pl.BlockSpec((rows, heads * w), lambda b, h, r: (b * nr + r, h))
    return pl.pallas_call(
        functools.partial(_gla_kernel, rows=rows, heads=heads),
        grid=grid,
        in_specs=[blk(GLA_DK), blk(GLA_DK), blk(GLA_DV),
                  pl.BlockSpec((1, heads, GLA_DV, rows), lambda b, h, r: (b, h, 0, r)),
                  blk(GLA_DK), _const_spec(g.shape), _const_spec(tri.shape)],
        out_specs=blk(GLA_DV),
        out_shape=jax.ShapeDtypeStruct((t, GLA_HEADS * GLA_DV), F32),
        scratch_shapes=[pltpu.VMEM((heads, GLA_DV, GLA_DK), F32)],
        compiler_params=pltpu.CompilerParams(
            dimension_semantics=("arbitrary", "arbitrary", "arbitrary"),
            vmem_limit_bytes=VMEM_LIMIT),
        name="gla",
    )(gq, gk, gv, gvt, la, g, tri)


def _mla_kernel(q_ref, k_ref, vt_ref, o_ref, s0_ref, *, heads):
    tile = MLA_TILE
    qi = pl.program_id(2)

    def scores(hh, kb):
        start = pl.multiple_of(kb * tile, tile)
        return _dot_nt(k_ref[0, hh, pl.ds(start, tile), :], q_ref[0, hh])

    def update(hh, kb, s, carry, masked):
        m, l, acc = carry
        if masked:
            key_i = lax.broadcasted_iota(jnp.int32, s.shape, 0)
            qry_i = lax.broadcasted_iota(jnp.int32, s.shape, 1)
            s = jnp.where(key_i <= qry_i, s, NEG_BIG)
        m_new = jnp.maximum(m, jnp.max(s, axis=0, keepdims=True))
        p = jnp.exp2(s - m_new)
        alpha = jnp.exp2(m - m_new)
        l = alpha * l + jnp.sum(p, axis=0, keepdims=True)
        acc = alpha * acc + _dot(vt_ref[0, hh, kb], p.astype(BF16))
        return m_new, l, acc

    def block(kb, carry, masked):
        out = []
        s_cur = s0_ref[...]
        for hh in range(heads):
            s_next = None
            if hh + 1 < heads:
                s_next = scores(hh + 1, kb)
            elif not masked:
                s_next = scores(0, kb + 1)
            out.append(update(hh, kb, s_cur, carry[hh], masked))
            if hh + 1 == heads and not masked:
                s0_ref[...] = s_next
            s_cur = s_next
        return tuple(out)

    one = (jnp.full((1, tile), NEG_BIG, F32), jnp.zeros((1, tile), F32), jnp.zeros((MLA_V, tile), F32))
    s0_ref[...] = scores(0, 0)
    carry = lax.fori_loop(0, qi, lambda kb, cr: block(kb, cr, False), (one,) * heads)
    carry = block(qi, carry, True)
    for hh in range(heads):
        _, l, acc = carry[hh]
        o_ref[:, hh * MLA_V:(hh + 1) * MLA_V] = (acc / l).T.astype(BF16)


def _mla(mq, mk, mvt, batch, seq, heads):
    tile = MLA_TILE
    nq = seq // tile
    grid = (batch, MLA_HEADS // heads, nq)
    return pl.pallas_call(
        functools.partial(_mla_kernel, heads=heads),
        grid=grid,
        in_specs=[pl.BlockSpec((1, heads, tile, MLA_QK_PAD), lambda b, g, i: (b, g, i, 0)),
                  pl.BlockSpec((1, heads, seq, MLA_QK_PAD), lambda b, g, i: (b, g, 0, 0)),
                  pl.BlockSpec((1, heads, nq, MLA_V, tile), lambda b, g, i: (b, g, 0, 0, 0))],
        out_specs=pl.BlockSpec((tile, heads * MLA_V), lambda b, g, i: (b * nq + i, g)),
        out_shape=jax.ShapeDtypeStruct((batch * seq, MLA_HEADS * MLA_V), BF16),
        scratch_shapes=[pltpu.VMEM((tile, tile), F32)],
        compiler_params=pltpu.CompilerParams(
            dimension_semantics=("arbitrary", "arbitrary", "arbitrary"),
            vmem_limit_bytes=VMEM_LIMIT),
        name="mla",
    )(mq, mk, mvt)


def _merge_kernel(x_ref, on_ref, om_ref, wg_ref, wgo_ref, wmo_ref, wout_ref, g_ref, b_ref, h_ref):
    x = x_ref[...]
    xb = x.astype(BF16)
    og = _dot(xb, wg_ref[:, 0:D_MODEL])
    gated = (on_ref[...] * (og * _sigmoid(og))).astype(BF16)
    y_gla = _dot(gated, wgo_ref[...])
    y_mla = _dot(om_ref[...], wmo_ref[...])
    ga = _sigmoid(_dot(xb, wg_ref[:, D_MODEL:2 * D_MODEL]))
    gb = _sigmoid(_dot(xb, wg_ref[:, 2 * D_MODEL:3 * D_MODEL]))
    mixed = _dot((ga * y_gla + gb * y_mla).astype(BF16), wout_ref[...])
    h_ref[...] = _layernorm(DN_ALPHA * x + mixed, g_ref[...], b_ref[...])


def _merge(x2, o_norm, o_mla, wg, wgo, wmo, wout, g, b, tm):
    t = x2.shape[0]
    row = pl.BlockSpec((tm, D_MODEL), lambda i: (i, 0))
    return pl.pallas_call(
        _merge_kernel,
        grid=(t // tm,),
        in_specs=[row, row, row, _const_spec(wg.shape), _const_spec(wgo.shape),
                  _const_spec(wmo.shape), _const_spec(wout.shape),
                  _const_spec(g.shape), _const_spec(b.shape)],
        out_specs=row,
        out_shape=jax.ShapeDtypeStruct((t, D_MODEL), F32),
        compiler_params=pltpu.CompilerParams(
            dimension_semantics=("arbitrary",), vmem_limit_bytes=VMEM_LIMIT),
        name="merge",
    )(x2, o_norm, o_mla, wg, wgo, wmo, wout, g, b)


FFN_CHUNK = 256
FFN_NCHUNK = D_FF // FFN_CHUNK
HALO = 16


def _ffn_kernel(h_ref, halo_ref, wug_ref, wuv_ref, cwg_ref, cwv_ref, cbg_ref, cbv_ref, wd_ref,
                g_ref, b_ref, o_ref, xs_ref, acc_ref, ug_ref, uv_ref, *, tm, tiles_per_seq):
    i = pl.program_id(0)
    h = h_ref[...]
    seq_start = (i % tiles_per_seq) == 0
    xs_ref[0:HALO, :] = jnp.where(seq_start, 0.0, halo_ref[...]).astype(BF16)
    xs_ref[HALO:HALO + tm, :] = h.astype(BF16)
    acc_ref[...] = jnp.zeros_like(acc_ref)

    def up(j, slot):
        xs = xs_ref[...]
        ug_ref[slot] = _dot(xs, wug_ref[j])
        uv_ref[slot] = _dot(xs, wuv_ref[j])

    def conv(u_ref, slot, cw, cb):
        return (cw[2:3, :] * u_ref[slot, pl.ds(HALO, tm), :]
                + cw[1:2, :] * u_ref[slot, pl.ds(HALO - 1, tm), :]
                + cw[0:1, :] * u_ref[slot, pl.ds(HALO - 2, tm), :] + cb)

    up(0, 0)
    for j in range(FFN_NCHUNK):
        slot = j % 2
        if j + 1 < FFN_NCHUNK:
            up(j + 1, 1 - slot)
        a = conv(ug_ref, slot, cwg_ref[j], cbg_ref[j])
        v = conv(uv_ref, slot, cwv_ref[j], cbv_ref[j])
        f = (a * _sigmoid(a) * v).astype(BF16)
        acc_ref[...] += _dot(f, wd_ref[j])
    o_ref[...] = _layernorm(DN_ALPHA * h + acc_ref[...], g_ref[...], b_ref[...])


def _ffn(h1, wug, wuv, cwg, cwv, cbg, cbv, wd, g, b, seq, tm):
    t = h1.shape[0]
    row = pl.BlockSpec((tm, D_MODEL), lambda i: (i, 0))
    halo = pl.BlockSpec((HALO, D_MODEL), lambda i: (jnp.maximum(i * (tm // HALO) - 1, 0), 0))
    return pl.pallas_call(
        functools.partial(_ffn_kernel, tm=tm, tiles_per_seq=seq // tm),
        grid=(t // tm,),
        in_specs=[row, halo, _const_spec(wug.shape), _const_spec(wuv.shape),
                  _const_spec(cwg.shape), _const_spec(cwv.shape),
                  _const_spec(cbg.shape), _const_spec(cbv.shape), _const_spec(wd.shape),
                  _const_spec(g.shape), _const_spec(b.shape)],
        out_specs=row,
        out_shape=jax.ShapeDtypeStruct((t, D_MODEL), F32),
        scratch_shapes=[pltpu.VMEM((HALO + tm, D_MODEL), BF16), pltpu.VMEM((tm, D_MODEL), F32),
                        pltpu.VMEM((2, HALO + tm, FFN_CHUNK), F32),
                        pltpu.VMEM((2, HALO + tm, FFN_CHUNK), F32)],
        compiler_params=pltpu.CompilerParams(
            dimension_semantics=("arbitrary",), vmem_limit_bytes=VMEM_LIMIT),
        name="ffn",
    )(h1, h1, wug, wuv, cwg, cwv, cbg, cbv, wd, g, b)


def _pack_weights(w_in, gla_w_gate_up, mla_w_uq, mla_w_ukv):
    o = 0
    parts = {}
    for name, n in (("q", 512), ("k", 512), ("v", 1024), ("r", GLA_RANK), ("og", 1024),
                    ("cq", MLA_Q_RANK), ("ckv", MLA_KV_RANK), ("kr", MLA_ROPE),
                    ("ga", D_MODEL), ("gb", D_MODEL)):
        parts[name] = w_in[:, o:o + n]
        o += n
    kr = parts["kr"]
    kr4 = jnp.concatenate([kr, kr], axis=1)
    r_pad = jnp.pad(parts["r"], ((0, 0), (0, LANES - GLA_RANK)))
    w_all = jnp.concatenate([parts["q"], parts["k"], parts["v"], parts["cq"], parts["ckv"],
                             kr4, r_pad], axis=1).astype(BF16)
    w_gates = jnp.concatenate([parts["og"], parts["ga"], parts["gb"]], axis=1).astype(BF16)

    wuq = mla_w_uq.reshape(MLA_Q_RANK, MLA_HEADS, MLA_NOPE + MLA_ROPE)
    nope = wuq[:, :, :MLA_NOPE].reshape(MLA_Q_RANK, MLA_HEADS * MLA_NOPE)
    rp = wuq[:, :, MLA_NOPE:]
    rope4 = jnp.concatenate([rp, rp], axis=2).reshape(MLA_Q_RANK, MLA_HEADS * LANES)
    wuq_p = jnp.concatenate([nope, rope4], axis=1).astype(BF16)
    wgu = jnp.pad(gla_w_gate_up, ((0, LANES - GLA_RANK), (0, 0))).astype(BF16)
    return w_all, w_gates, wuq_p, mla_w_ukv.astype(BF16), wgu


def kernel(x, positions, w_in, gla_w_gate_up, gla_b_gate, gla_norm_g, w_gla_o, mla_q_norm_g, mla_w_uq,
           mla_kv_norm_g, mla_w_ukv, w_mla_o, w_out, ln1_g, ln1_b, w_up, conv_w, conv_b, w_down,
           ln2_g, ln2_b):
    batch, seq, d = x.shape
    t = batch * seq
    depth = w_in.shape[0]
    half = MLA_ROPE // 2
    inv_freq = ROPE_THETA ** (-jnp.arange(half, dtype=F32) / half)
    invf = jnp.tile(inv_freq, LANES // half).reshape(1, LANES)
    pos_b = jnp.broadcast_to(positions.astype(F32).reshape(t, 1), (t, LANES))
    ridx = jnp.arange(GLA_TRI)
    tri = ((ridx[:, None] // GLA_SUPER == ridx[None, :] // GLA_SUPER) &
           (ridx[None, :] <= ridx[:, None])).astype(BF16)

    h = x.reshape(t, d)
    for l in range(depth):
        w_all, w_gates, wuq, wukv, wgu = _pack_weights(w_in[l], gla_w_gate_up[l], mla_w_uq[l], mla_w_ukv[l])
        gq, gk, gv, gvt, la, mq, mk, mvt = _inproj(
            h, pos_b, invf, w_all, wuq, wukv, wgu, gla_b_gate[l].reshape(1, -1),
            mla_q_norm_g[l].reshape(1, -1), mla_kv_norm_g[l].reshape(1, -1), batch, seq, tm=256)
        o_norm = _gla(gq, gk, gv, gvt, la, gla_norm_g[l].reshape(1, -1), tri, batch, seq,
                      rows=512, heads=2)
        o_mla = _mla(mq, mk, mvt, batch, seq, heads=MLA_HEADS_PER_STEP)
        h1 = _merge(h, o_norm, o_mla, w_gates, w_gla_o[l].astype(BF16), w_mla_o[l].astype(BF16),
                    w_out[l].astype(BF16), ln1_g[l].reshape(1, -1), ln1_b[l].reshape(1, -1), tm=512)

        wu = w_up[l].astype(BF16)
        to_chunks = lambda a: a.reshape(a.shape[0], FFN_NCHUNK, FFN_CHUNK).transpose(1, 0, 2)
        wug, wuv = to_chunks(wu[:, :D_FF]), to_chunks(wu[:, D_FF:])
        cwg, cwv = to_chunks(conv_w[l][:, :D_FF]), to_chunks(conv_w[l][:, D_FF:])
        cb = conv_b[l].reshape(1, -1)
        cbg, cbv = to_chunks(cb[:, :D_FF]), to_chunks(cb[:, D_FF:])
        wd = w_down[l].astype(BF16).reshape(FFN_NCHUNK, FFN_CHUNK, D_MODEL)
        h = _ffn(h1, wug, wuv, cwg, cwv, cbg, cbv, wd, ln2_g[l].reshape(1, -1),
                 ln2_b[l].reshape(1, -1), seq, tm=512)
    return h.reshape(batch, seq, d)
```

```python
import functools

import jax
import jax.numpy as jnp
from jax import lax
from jax.experimental import pallas as pl
from jax.experimental.pallas import tpu as pltpu

F32 = jnp.float32
BF16 = jnp.bfloat16

D_MODEL = 1024
GLA_HEADS = 4
GLA_DK = 128
GLA_DV = 256
GLA_RANK = 16
GLA_TAU = 16.0
GLA_CHUNK = 64
MLA_HEADS = 8
MLA_Q_RANK = 384
MLA_KV_RANK = 256
MLA_NOPE = 128
MLA_ROPE = 64
MLA_V = 128
ROPE_THETA = 10000.0
D_FF = 2816
DN_ALPHA = 2.0 ** 0.25
LN_EPS = 1e-5
RMS_EPS = 1e-6

LANES = 128
MLA_QK_PAD = 256
MLA_TILE = 512
MLA_HEADS_PER_STEP = 4
NEG_BIG = -1e30
LOG2_E = 1.4426950408889634

C_Q, C_K, C_V = 0, 512, 1024
C_CQ = 2048
C_CKV = C_CQ + MLA_Q_RANK
C_KR = C_CKV + MLA_KV_RANK
C_R = C_KR + LANES
C_END = C_R + LANES

VMEM_LIMIT = 56 * 1024 * 1024


def _sigmoid(x):
    return 1.0 / (1.0 + jnp.exp(-x))


def _dot(a, b):
    return jnp.dot(a, b, preferred_element_type=F32)


def _dot_nt(a, b):
    return lax.dot_general(a, b, (((1,), (1,)), ((), ())), preferred_element_type=F32)


def _rmsnorm(x, g):
    ms = jnp.mean(x * x, axis=-1, keepdims=True)
    return x * lax.rsqrt(ms + RMS_EPS) * g


def _layernorm(x, g, b):
    mu = jnp.mean(x, axis=-1, keepdims=True)
    xc = x - mu
    var = jnp.mean(xc * xc, axis=-1, keepdims=True)
    return xc * lax.rsqrt(var + LN_EPS) * g + b


def _const_spec(shape):
    nd = len(shape)
    return pl.BlockSpec(shape, lambda *_: (0,) * nd, pipeline_mode=pl.Buffered(1))


def _inproj_kernel(x_ref, pos_ref, invf_ref, w_ref, wuq_ref, wukv_ref, wgu_ref, bg_ref,
                   gqn_ref, gkvn_ref,
                   gq_ref, gk_ref, gv_ref, gvt_ref, la_ref, mq_ref, mk_ref, mvt_ref):
    xb = x_ref[...].astype(BF16)

    def proj(lo, hi):
        return _dot(xb, w_ref[:, lo:hi])

    gq_ref[...] = proj(C_Q, C_K) * (GLA_DK ** -0.5)
    gk_ref[...] = proj(C_K, C_V)
    gv = proj(C_V, C_CQ)
    gv_ref[...] = gv.astype(BF16)
    for h in range(GLA_HEADS):
        gvt_ref[0, h, :, :] = gv[:, h * GLA_DV:(h + 1) * GLA_DV].T.astype(BF16)

    r = proj(C_R, C_END).astype(BF16)
    logit = _dot(r, wgu_ref[...]) + bg_ref[...]
    log_sig = jnp.minimum(logit, 0.0) - jnp.log(1.0 + jnp.exp(-jnp.abs(logit)))
    la_ref[...] = log_sig / GLA_TAU

    ang = pos_ref[...] * invf_ref[...]
    lane = lax.broadcasted_iota(jnp.int32, ang.shape, 1)
    cos = jnp.cos(ang)
    sin = jnp.sin(ang)
    cos_a = jnp.where(lane < MLA_ROPE, cos, 0.0)
    sin_a = jnp.where(lane < MLA_ROPE // 2, -sin, jnp.where(lane < MLA_ROPE, sin, 0.0))

    def rope(blk):
        return blk * cos_a + pltpu.roll(blk, MLA_ROPE // 2, 1) * sin_a

    scale = (MLA_NOPE + MLA_ROPE) ** -0.5 * LOG2_E
    cq = _rmsnorm(proj(C_CQ, C_CKV), gqn_ref[...]).astype(BF16)
    q = _dot(cq, wuq_ref[...])
    nope_w = MLA_HEADS * MLA_NOPE
    for h in range(MLA_HEADS):
        mq_ref[0, h, :, 0:LANES] = (q[:, h * LANES:(h + 1) * LANES] * scale).astype(BF16)
        blk = q[:, nope_w + h * LANES: nope_w + (h + 1) * LANES]
        mq_ref[0, h, :, LANES:2 * LANES] = (rope(blk) * scale).astype(BF16)

    ckv = _rmsnorm(proj(C_CKV, C_KR), gkvn_ref[...]).astype(BF16)
    kv = _dot(ckv, wukv_ref[...])
    kr = rope(proj(C_KR, C_R)).astype(BF16)
    for h in range(MLA_HEADS):
        base = h * (MLA_NOPE + MLA_V)
        mk_ref[0, h, :, 0:LANES] = kv[:, base:base + MLA_NOPE].astype(BF16)
        mk_ref[0, h, :, LANES:2 * LANES] = kr
        mvt_ref[0, h, 0, :, :] = kv[:, base + MLA_NOPE:base + MLA_NOPE + MLA_V].T.astype(BF16)


def _inproj(x2, pos_b, invf, w_all, wuq, wukv, wgu, bg, gqn, gkvn, batch, seq, tm):
    t = x2.shape[0]
    nj = seq // tm
    grid = (t // tm,)
    row = lambda w: pl.BlockSpec((tm, w), lambda i: (i, 0))
    head = lambda w: pl.BlockSpec((1, MLA_HEADS, tm, w), lambda i: (i // nj, 0, i % nj, 0))
    per_tile = MLA_TILE // tm
    vt_spec = pl.BlockSpec((1, MLA_HEADS, 1, MLA_V, tm),
                           lambda i: (i // nj, 0, (i % nj) // per_tile, 0, (i % nj) % per_tile))
    gvt_spec = pl.BlockSpec((1, GLA_HEADS, GLA_DV, tm), lambda i: (i // nj, 0, 0, i % nj))
    out_shape = (
        jax.ShapeDtypeStruct((t, GLA_HEADS * GLA_DK), F32),
        jax.ShapeDtypeStruct((t, GLA_HEADS * GLA_DK), F32),
        jax.ShapeDtypeStruct((t, GLA_HEADS * GLA_DV), BF16),
        jax.ShapeDtypeStruct((batch, GLA_HEADS, GLA_DV, seq), BF16),
        jax.ShapeDtypeStruct((t, GLA_HEADS * GLA_DK), F32),
        jax.ShapeDtypeStruct((batch, MLA_HEADS, seq, MLA_QK_PAD), BF16),
        jax.ShapeDtypeStruct((batch, MLA_HEADS, seq, MLA_QK_PAD), BF16),
        jax.ShapeDtypeStruct((batch, MLA_HEADS, seq // MLA_TILE, MLA_V, MLA_TILE), BF16),
    )
    return pl.pallas_call(
        _inproj_kernel,
        grid=grid,
        in_specs=[row(D_MODEL), row(LANES), _const_spec(invf.shape), _const_spec(w_all.shape),
                  _const_spec(wuq.shape), _const_spec(wukv.shape), _const_spec(wgu.shape),
                  _const_spec(bg.shape), _const_spec(gqn.shape), _const_spec(gkvn.shape)],
        out_specs=(row(512), row(512), row(1024), gvt_spec, row(512),
                   head(MLA_QK_PAD), head(MLA_QK_PAD), vt_spec),
        out_shape=out_shape,
        compiler_params=pltpu.CompilerParams(
            dimension_semantics=("arbitrary",), vmem_limit_bytes=VMEM_LIMIT),
        name="inproj",
    )(x2, pos_b, invf, w_all, wuq, wukv, wgu, bg, gqn, gkvn)


GLA_SUPER = 2 * GLA_CHUNK
GLA_TRI = 2 * GLA_SUPER


def _split2_dot(tri, x):
    hi = x.astype(BF16)
    lo = (x - hi.astype(F32)).astype(BF16)
    return _dot(tri, hi) + _dot(tri, lo)


def _gla_kernel(q_ref, k_ref, v_ref, vt_ref, la_ref, g_ref, tri_ref, o_ref, st_ref, *, rows, heads):
    @pl.when(pl.program_id(2) == 0)
    def _():
        st_ref[...] = jnp.zeros_like(st_ref)

    c = GLA_SUPER
    tri = tri_ref[...]
    b_all = jnp.concatenate([_split2_dot(tri, la_ref[r:r + GLA_TRI, :])
                             for r in range(0, rows, GLA_TRI)], axis=0)
    row_i = lax.broadcasted_iota(jnp.int32, (c, c), 0)
    col_i = lax.broadcasted_iota(jnp.int32, (c, c), 1)
    causal = col_i <= row_i
    g = g_ref[...]
    items = [(n, hh) for n in range(rows // c) for hh in range(heads)]

    def prep(n, hh):
        sl = slice(n * c, (n + 1) * c)
        lanes = slice(hh * GLA_DK, (hh + 1) * GLA_DK)
        b = b_all[sl, lanes]
        b_mid = b[c // 2 - 1:c // 2, :]
        b_end = b[c - 1:c, :]
        q = q_ref[sl, lanes]
        k = k_ref[sl, lanes]
        return dict(
            sl=sl,
            q_mid=(q * jnp.exp(b - b_mid)).astype(BF16),
            k_mid=(k * jnp.exp(b_mid - b)).astype(BF16),
            q_in=(q * jnp.exp(b)).astype(BF16),
            k_end=(k * jnp.exp(b_end - b)).astype(BF16),
            decay=jnp.exp(b_end))

    state = [st_ref[hh] for hh in range(heads)]
    cur = prep(*items[0])
    cur_s = _dot_nt(cur["q_mid"], cur["k_mid"])
    for i, (n, hh) in enumerate(items):
        if i + 1 < len(items):
            nxt = prep(*items[i + 1])
            nxt_s = _dot_nt(nxt["q_mid"], nxt["k_mid"])
        sl = cur["sl"]
        vcol = slice(hh * GLA_DV, (hh + 1) * GLA_DV)
        att = jnp.where(causal, cur_s, 0.0).astype(BF16)
        o = _dot(att, v_ref[sl, vcol]) + _dot_nt(cur["q_in"], state[hh].astype(BF16))
        state[hh] = state[hh] * cur["decay"] + _dot(vt_ref[0, hh, :, sl], cur["k_end"])
        o_ref[sl, vcol] = _rmsnorm(o, g)
        if i + 1 < len(items):
            cur, cur_s = nxt, nxt_s
    for hh in range(heads):
        st_ref[hh] = state[hh]


def _gla(gq, gk, gv, gvt, la, g, tri, batch, seq, rows, heads):
    t = gq.shape[0]
    nr = seq // rows
    grid = (batch, GLA_HEADS // heads, nr)
    blk = lambda w: pl.BlockSpec((rows, heads * w), lambda b, h, r: (b * nr + r, h))
    return pl.pallas_call(
        functools.partial(_gla_kernel, rows=rows, heads=heads),
        grid=grid,
        in_specs=[blk(GLA_DK), blk(GLA_DK), blk(GLA_DV),
                  pl.BlockSpec((1, heads, GLA_DV, rows), lambda b, h, r: (b, h, 0, r)),
                  blk(GLA_DK), _const_spec(g.shape), _const_spec(tri.shape)],
        out_specs=blk(GLA_DV),
        out_shape=jax.ShapeDtypeStruct((t, GLA_HEADS * GLA_DV), F32),
        scratch_shapes=[pltpu.VMEM((heads, GLA_DV, GLA_DK), F32)],
        compiler_params=pltpu.CompilerParams(
            dimension_semantics=("arbitrary", "arbitrary", "arbitrary"),
            vmem_limit_bytes=VMEM_LIMIT),
        name="gla",
    )(gq, gk, gv, gvt, la, g, tri)


def _mla_kernel(q_ref, k_ref, vt_ref, o_ref, *, heads, nq):
    tile = MLA_TILE

    def scores(qi, kb, hh):
        return _dot_nt(k_ref[0, hh, kb * tile:(kb + 1) * tile, :],
                       q_ref[0, hh, qi * tile:(qi + 1) * tile, :])

    def update(kb, hh, s, carry, masked):
        m, l, acc = carry
        if masked:
            key_i = lax.broadcasted_iota(jnp.int32, s.shape, 0)
            qry_i = lax.broadcasted_iota(jnp.int32, s.shape, 1)
            s = jnp.where(key_i <= qry_i, s, NEG_BIG)
        m_new = jnp.maximum(m, jnp.max(s, axis=0, keepdims=True))
        p = jnp.exp2(s - m_new)
        alpha = jnp.exp2(m - m_new)
        l = alpha * l + jnp.sum(p, axis=0, keepdims=True)
        acc = alpha * acc + _dot(vt_ref[0, hh, kb], p.astype(BF16))
        return m_new, l, acc

    init = (jnp.full((1, tile), NEG_BIG, F32), jnp.zeros((1, tile), F32), jnp.zeros((MLA_V, tile), F32))
    items = [(qi, kb, hh) for qi in range(nq) for kb in range(qi + 1) for hh in range(heads)]
    carry = {}
    s_cur = scores(*items[0])
    for idx, (qi, kb, hh) in enumerate(items):
        s_next = scores(*items[idx + 1]) if idx + 1 < len(items) else None
        diagonal = kb == qi
        c = update(kb, hh, s_cur, init if kb == 0 else carry[hh], diagonal)
        carry[hh] = c
        if diagonal:
            _, l, acc = c
            o_ref[qi * tile:(qi + 1) * tile, hh * MLA_V:(hh + 1) * MLA_V] = (acc / l).T.astype(BF16)
        s_cur = s_next


def _mla(mq, mk, mvt, batch, seq, heads):
    nq = seq // MLA_TILE
    grid = (batch, MLA_HEADS // heads)
    return pl.pallas_call(
        functools.partial(_mla_kernel, heads=heads, nq=nq),
        grid=grid,
        in_specs=[pl.BlockSpec((1, heads, seq, MLA_QK_PAD), lambda b, g: (b, g, 0, 0)),
                  pl.BlockSpec((1, heads, seq, MLA_QK_PAD), lambda b, g: (b, g, 0, 0)),
                  pl.BlockSpec((1, heads, nq, MLA_V, MLA_TILE), lambda b, g: (b, g, 0, 0, 0))],
        out_specs=pl.BlockSpec((seq, heads * MLA_V), lambda b, g: (b, g)),
        out_shape=jax.ShapeDtypeStruct((batch * seq, MLA_HEADS * MLA_V), BF16),
        compiler_params=pltpu.CompilerParams(
            dimension_semantics=("arbitrary", "arbitrary"), vmem_limit_bytes=VMEM_LIMIT),
        name="mla",
    )(mq, mk, mvt)


def _merge_kernel(x_ref, on_ref, om_ref, wg_ref, wgo_ref, wmo_ref, wout_ref, g_ref, b_ref, h_ref, *, sub):
    for r0 in range(0, x_ref.shape[0], sub):
        rows = slice(r0, r0 + sub)
        x = x_ref[rows, :]
        xb = x.astype(BF16)
        og = _dot(xb, wg_ref[:, 0:D_MODEL])
        gated = (on_ref[rows, :] * (og * _sigmoid(og))).astype(BF16)
        y_gla = _dot(gated, wgo_ref[...])
        y_mla = _dot(om_ref[rows, :], wmo_ref[...])
        ga = _sigmoid(_dot(xb, wg_ref[:, D_MODEL:2 * D_MODEL]))
        gb = _sigmoid(_dot(xb, wg_ref[:, 2 * D_MODEL:3 * D_MODEL]))
        mixed = _dot((ga * y_gla + gb * y_mla).astype(BF16), wout_ref[...])
        h_ref[rows, :] = _layernorm(DN_ALPHA * x + mixed, g_ref[...], b_ref[...])


MERGE_SUB = 512


def _merge(x2, o_norm, o_mla, wg, wgo, wmo, wout, g, b, tm):
    t = x2.shape[0]
    row = pl.BlockSpec((tm, D_MODEL), lambda i: (i, 0))
    return pl.pallas_call(
        functools.partial(_merge_kernel, sub=MERGE_SUB),
        grid=(t // tm,),
        in_specs=[row, row, row, _const_spec(wg.shape), _const_spec(wgo.shape),
                  _const_spec(wmo.shape), _const_spec(wout.shape),
                  _const_spec(g.shape), _const_spec(b.shape)],
        out_specs=row,
        out_shape=jax.ShapeDtypeStruct((t, D_MODEL), F32),
        compiler_params=pltpu.CompilerParams(
            dimension_semantics=("arbitrary",), vmem_limit_bytes=VMEM_LIMIT),
        name="merge",
    )(x2, o_norm, o_mla, wg, wgo, wmo, wout, g, b)


FFN_CHUNK = 256
FFN_NCHUNK = D_FF // FFN_CHUNK
HALO = 16


def _ffn_kernel(h_ref, halo_ref, wug_ref, wuv_ref, cwg_ref, cwv_ref, cbg_ref, cbv_ref, wd_ref,
                g_ref, b_ref, o_ref, xs_ref, f_ref, ug_ref, uv_ref, *, tm, tiles_per_seq):
    i = pl.program_id(0)
    h = h_ref[...]
    seq_start = (i % tiles_per_seq) == 0
    xs_ref[0:HALO, :] = jnp.where(seq_start, 0.0, halo_ref[...]).astype(BF16)
    xs_ref[HALO:HALO + tm, :] = h.astype(BF16)

    def up(j, slot):
        xs = xs_ref[...]
        ug_ref[slot] = _dot(xs, wug_ref[j])
        uv_ref[slot] = _dot(xs, wuv_ref[j])

    def conv(u_ref, slot, cw, cb):
        return (cw[2:3, :] * u_ref[slot, pl.ds(HALO, tm), :]
                + cw[1:2, :] * u_ref[slot, pl.ds(HALO - 1, tm), :]
                + cw[0:1, :] * u_ref[slot, pl.ds(HALO - 2, tm), :] + cb)

    up(0, 0)
    for j in range(FFN_NCHUNK):
        slot = j % 2
        if j + 1 < FFN_NCHUNK:
            up(j + 1, 1 - slot)
        a = conv(ug_ref, slot, cwg_ref[j], cbg_ref[j])
        v = conv(uv_ref, slot, cwv_ref[j], cbv_ref[j])
        f_ref[:, j * FFN_CHUNK:(j + 1) * FFN_CHUNK] = (a * _sigmoid(a) * v).astype(BF16)
    o_ref[...] = _layernorm(DN_ALPHA * h + _dot(f_ref[...], wd_ref[...]), g_ref[...], b_ref[...])


def _ffn(h1, wug, wuv, cwg, cwv, cbg, cbv, wd, g, b, seq, tm):
    t = h1.shape[0]
    row = pl.BlockSpec((tm, D_MODEL), lambda i: (i, 0))
    halo = pl.BlockSpec((HALO, D_MODEL), lambda i: (jnp.maximum(i * (tm // HALO) - 1, 0), 0))
    return pl.pallas_call(
        functools.partial(_ffn_kernel, tm=tm, tiles_per_seq=seq // tm),
        grid=(t // tm,),
        in_specs=[row, halo, _const_spec(wug.shape), _const_spec(wuv.shape),
                  _const_spec(cwg.shape), _const_spec(cwv.shape),
                  _const_spec(cbg.shape), _const_spec(cbv.shape), _const_spec(wd.shape),
                  _const_spec(g.shape), _const_spec(b.shape)],
        out_specs=row,
        out_shape=jax.ShapeDtypeStruct((t, D_MODEL), F32),
        scratch_shapes=[pltpu.VMEM((HALO + tm, D_MODEL), BF16), pltpu.VMEM((tm, D_FF), BF16),
                        pltpu.VMEM((2, HALO + tm, FFN_CHUNK), F32),
                        pltpu.VMEM((2, HALO + tm, FFN_CHUNK), F32)],
        compiler_params=pltpu.CompilerParams(
            dimension_semantics=("arbitrary",), vmem_limit_bytes=VMEM_LIMIT),
        name="ffn",
    )(h1, h1, wug, wuv, cwg, cwv, cbg, cbv, wd, g, b)


def _pack_weights(w_in, gla_w_gate_up, mla_w_uq, mla_w_ukv):
    o = 0
    parts = {}
    for name, n in (("q", 512), ("k", 512), ("v", 1024), ("r", GLA_RANK), ("og", 1024),
                    ("cq", MLA_Q_RANK), ("ckv", MLA_KV_RANK), ("kr", MLA_ROPE),
                    ("ga", D_MODEL), ("gb", D_MODEL)):
        parts[name] = w_in[:, o:o + n]
        o += n
    kr = parts["kr"]
    kr4 = jnp.concatenate([kr, kr], axis=1)
    r_pad = jnp.pad(parts["r"], ((0, 0), (0, LANES - GLA_RANK)))
    w_all = jnp.concatenate([parts["q"], parts["k"], parts["v"], parts["cq"], parts["ckv"],
                             kr4, r_pad], axis=1).astype(BF16)
    w_gates = jnp.concatenate([parts["og"], parts["ga"], parts["gb"]], axis=1).astype(BF16)

    wuq = mla_w_uq.reshape(MLA_Q_RANK, MLA_HEADS, MLA_NOPE + MLA_ROPE)
    nope = wuq[:, :, :MLA_NOPE].reshape(MLA_Q_RANK, MLA_HEADS * MLA_NOPE)
    rp = wuq[:, :, MLA_NOPE:]
    rope4 = jnp.concatenate([rp, rp], axis=2).reshape(MLA_Q_RANK, MLA_HEADS * LANES)
    wuq_p = jnp.concatenate([nope, rope4], axis=1).astype(BF16)
    wgu = jnp.pad(gla_w_gate_up, ((0, LANES - GLA_RANK), (0, 0))).astype(BF16)
    return w_all, w_gates, wuq_p, mla_w_ukv.astype(BF16), wgu


def kernel(x, positions, w_in, gla_w_gate_up, gla_b_gate, gla_norm_g, w_gla_o, mla_q_norm_g, mla_w_uq,
           mla_kv_norm_g, mla_w_ukv, w_mla_o, w_out, ln1_g, ln1_b, w_up, conv_w, conv_b, w_down,
           ln2_g, ln2_b):
    batch, seq, d = x.shape
    t = batch * seq
    depth = w_in.shape[0]
    half = MLA_ROPE // 2
    inv_freq = ROPE_THETA ** (-jnp.arange(half, dtype=F32) / half)
    invf = jnp.tile(inv_freq, LANES // half).reshape(1, LANES)
    pos_b = jnp.broadcast_to(positions.astype(F32).reshape(t, 1), (t, LANES))
    ridx = jnp.arange(GLA_TRI)
    tri = ((ridx[:, None] // GLA_SUPER == ridx[None, :] // GLA_SUPER) &
           (ridx[None, :] <= ridx[:, None])).astype(BF16)

    h = x.reshape(t, d)
    for l in range(depth):
        w_all, w_gates, wuq, wukv, wgu = _pack_weights(w_in[l], gla_w_gate_up[l], mla_w_uq[l], mla_w_ukv[l])
        gq, gk, gv, gvt, la, mq, mk, mvt = _inproj(
            h, pos_b, invf, w_all, wuq, wukv, wgu, gla_b_gate[l].reshape(1, -1),
            mla_q_norm_g[l].reshape(1, -1), mla_kv_norm_g[l].reshape(1, -1), batch, seq, tm=512)
        o_norm = _gla(gq, gk, gv, gvt, la, gla_norm_g[l].reshape(1, -1), tri, batch, seq,
                      rows=512, heads=2)
        o_mla = _mla(mq, mk, mvt, batch, seq, heads=MLA_HEADS_PER_STEP)
        h1 = _merge(h, o_norm, o_mla, w_gates, w_gla_o[l].astype(BF16), w_mla_o[l].astype(BF16),
                    w_out[l].astype(BF16), ln1_g[l].reshape(1, -1), ln1_b[l].reshape(1, -1), tm=2 * MERGE_SUB)

        wu = w_up[l].astype(BF16)
        to_chunks = lambda a: a.reshape(a.shape[0], FFN_NCHUNK, FFN_CHUNK).transpose(1, 0, 2)
        wug, wuv = to_chunks(wu[:, :D_FF]), to_chunks(wu[:, D_FF:])
        cwg, cwv = to_chunks(conv_w[l][:, :D_FF]), to_chunks(conv_w[l][:, D_FF:])
        cb = conv_b[l].reshape(1, -1)
        cbg, cbv = to_chunks(cb[:, :D_FF]), to_chunks(cb[:, D_FF:])
        wd = w_down[l].astype(BF16)
        h = _ffn(h1, wug, wuv, cwg, cwv, cbg, cbv, wd, ln2_g[l].reshape(1, -1),
                 ln2_b[l].reshape(1, -1), seq, tm=512)
    return h.reshape(batch, seq, d)
```

```python
import functools

import jax
import jax.numpy as jnp
from jax import lax
from jax.experimental import pallas as pl
from jax.experimental.pallas import tpu as pltpu

F32 = jnp.float32
BF16 = jnp.bfloat16

D_MODEL = 1024
GLA_HEADS = 4
GLA_DK = 128
GLA_DV = 256
GLA_RANK = 16
GLA_TAU = 16.0
GLA_CHUNK = 64
MLA_HEADS = 8
MLA_Q_RANK = 384
MLA_KV_RANK = 256
MLA_NOPE = 128
MLA_ROPE = 64
MLA_V = 128
ROPE_THETA = 10000.0
D_FF = 2816
DN_ALPHA = 2.0 ** 0.25
LN_EPS = 1e-5
RMS_EPS = 1e-6

LANES = 128
MLA_QK_PAD = 256
MLA_TILE = 512
MLA_HEADS_PER_STEP = 4
NEG_BIG = -1e30
LOG2_E = 1.4426950408889634

C_Q, C_K, C_V = 0, 512, 1024
C_CQ = 2048
C_CKV = C_CQ + MLA_Q_RANK
C_KR = C_CKV + MLA_KV_RANK
C_R = C_KR + LANES
C_END = C_R + LANES

VMEM_LIMIT = 56 * 1024 * 1024


def _sigmoid(x):
    return 1.0 / (1.0 + jnp.exp(-x))


def _dot(a, b):
    return jnp.dot(a, b, preferred_element_type=F32)


def _dot_nt(a, b):
    return lax.dot_general(a, b, (((1,), (1,)), ((), ())), preferred_element_type=F32)


def _rmsnorm(x, g):
    ms = jnp.mean(x * x, axis=-1, keepdims=True)
    return x * lax.rsqrt(ms + RMS_EPS) * g


def _layernorm(x, g, b):
    mu = jnp.mean(x, axis=-1, keepdims=True)
    xc = x - mu
    var = jnp.mean(xc * xc, axis=-1, keepdims=True)
    return xc * lax.rsqrt(var + LN_EPS) * g + b


def _const_spec(shape):
    nd = len(shape)
    return pl.BlockSpec(shape, lambda *_: (0,) * nd, pipeline_mode=pl.Buffered(1))


def _inproj_kernel(x_ref, pos_ref, invf_ref, w_ref, wuq_ref, wukv_ref, wgu_ref, bg_ref,
                   gqn_ref, gkvn_ref,
                   gq_ref, gk_ref, gv_ref, gvt_ref, la_ref, mq_ref, mk_ref, mvt_ref):
    xb = x_ref[...].astype(BF16)

    def proj(lo, hi):
        return _dot(xb, w_ref[:, lo:hi])

    gq_ref[...] = (proj(C_Q, C_K) * (GLA_DK ** -0.5)).astype(BF16)
    gk_ref[...] = proj(C_K, C_V).astype(BF16)
    gv = proj(C_V, C_CQ)
    gv_ref[...] = gv.astype(BF16)
    for h in range(GLA_HEADS):
        gvt_ref[0, h, :, :] = gv[:, h * GLA_DV:(h + 1) * GLA_DV].T.astype(BF16)

    r = proj(C_R, C_END).astype(BF16)
    logit = _dot(r, wgu_ref[...]) + bg_ref[...]
    log_sig = jnp.minimum(logit, 0.0) - jnp.log(1.0 + jnp.exp(-jnp.abs(logit)))
    la_ref[...] = log_sig / GLA_TAU

    ang = pos_ref[...] * invf_ref[...]
    lane = lax.broadcasted_iota(jnp.int32, ang.shape, 1)
    cos = jnp.cos(ang)
    sin = jnp.sin(ang)
    cos_a = jnp.where(lane < MLA_ROPE, cos, 0.0)
    sin_a = jnp.where(lane < MLA_ROPE // 2, -sin, jnp.where(lane < MLA_ROPE, sin, 0.0))

    def rope(blk):
        return blk * cos_a + pltpu.roll(blk, MLA_ROPE // 2, 1) * sin_a

    scale = (MLA_NOPE + MLA_ROPE) ** -0.5 * LOG2_E
    cq = _rmsnorm(proj(C_CQ, C_CKV), gqn_ref[...]).astype(BF16)
    q = _dot(cq, wuq_ref[...])
    nope_w = MLA_HEADS * MLA_NOPE
    for h in range(MLA_HEADS):
        mq_ref[0, h, :, 0:LANES] = (q[:, h * LANES:(h + 1) * LANES] * scale).astype(BF16)
        blk = q[:, nope_w + h * LANES: nope_w + (h + 1) * LANES]
        mq_ref[0, h, :, LANES:2 * LANES] = (rope(blk) * scale).astype(BF16)

    ckv = _rmsnorm(proj(C_CKV, C_KR), gkvn_ref[...]).astype(BF16)
    kv = _dot(ckv, wukv_ref[...])
    kr = rope(proj(C_KR, C_R)).astype(BF16)
    for h in range(MLA_HEADS):
        base = h * (MLA_NOPE + MLA_V)
        mk_ref[0, h, :, 0:LANES] = kv[:, base:base + MLA_NOPE].astype(BF16)
        mk_ref[0, h, :, LANES:2 * LANES] = kr
        mvt_ref[0, h, 0, :, :] = kv[:, base + MLA_NOPE:base + MLA_NOPE + MLA_V].T.astype(BF16)


def _inproj(x2, pos_b, invf, w_all, wuq, wukv, wgu, bg, gqn, gkvn, batch, seq, tm):
    t = x2.shape[0]
    nj = seq // tm
    grid = (t // tm,)
    row = lambda w: pl.BlockSpec((tm, w), lambda i: (i, 0))
    head = lambda w: pl.BlockSpec((1, MLA_HEADS, tm, w), lambda i: (i // nj, 0, i % nj, 0))
    per_tile = MLA_TILE // tm
    vt_spec = pl.BlockSpec((1, MLA_HEADS, 1, MLA_V, tm),
                           lambda i: (i // nj, 0, (i % nj) // per_tile, 0, (i % nj) % per_tile))
    gvt_spec = pl.BlockSpec((1, GLA_HEADS, GLA_DV, tm), lambda i: (i // nj, 0, 0, i % nj))
    out_shape = (
        jax.ShapeDtypeStruct((t, GLA_HEADS * GLA_DK), BF16),
        jax.ShapeDtypeStruct((t, GLA_HEADS * GLA_DK), BF16),
        jax.ShapeDtypeStruct((t, GLA_HEADS * GLA_DV), BF16),
        jax.ShapeDtypeStruct((batch, GLA_HEADS, GLA_DV, seq), BF16),
        jax.ShapeDtypeStruct((t, GLA_HEADS * GLA_DK), F32),
        jax.ShapeDtypeStruct((batch, MLA_HEADS, seq, MLA_QK_PAD), BF16),
        jax.ShapeDtypeStruct((batch, MLA_HEADS, seq, MLA_QK_PAD), BF16),
        jax.ShapeDtypeStruct((batch, MLA_HEADS, seq // MLA_TILE, MLA_V, MLA_TILE), BF16),
    )
    return pl.pallas_call(
        _inproj_kernel,
        grid=grid,
        in_specs=[row(D_MODEL), row(LANES), _const_spec(invf.shape), _const_spec(w_all.shape),
                  _const_spec(wuq.shape), _const_spec(wukv.shape), _const_spec(wgu.shape),
                  _const_spec(bg.shape), _const_spec(gqn.shape), _const_spec(gkvn.shape)],
        out_specs=(row(512), row(512), row(1024), gvt_spec, row(512),
                   head(MLA_QK_PAD), head(MLA_QK_PAD), vt_spec),
        out_shape=out_shape,
        compiler_params=pltpu.CompilerParams(
            dimension_semantics=("arbitrary",), vmem_limit_bytes=VMEM_LIMIT),
        name="inproj",
    )(x2, pos_b, invf, w_all, wuq, wukv, wgu, bg, gqn, gkvn)


GLA_ROWS = 1024
GLA_HEADS_PER_STEP = 4
GLA_SUPER = 2 * GLA_CHUNK
GLA_TRI = 2 * GLA_SUPER


def _split2_dot(tri, x):
    hi = x.astype(BF16)
    lo = (x - hi.astype(F32)).astype(BF16)
    return _dot(tri, hi) + _dot(tri, lo)


def _gla_kernel(q_ref, k_ref, v_ref, vt_ref, la_ref, g_ref, tri_ref, o_ref, st_ref, *, rows, heads):
    @pl.when(pl.program_id(2) == 0)
    def _():
        st_ref[...] = jnp.zeros_like(st_ref)

    c = GLA_SUPER
    tri = tri_ref[...]
    b_all = jnp.concatenate([_split2_dot(tri, la_ref[r:r + GLA_TRI, :])
                             for r in range(0, rows, GLA_TRI)], axis=0)
    row_i = lax.broadcasted_iota(jnp.int32, (c, c), 0)
    col_i = lax.broadcasted_iota(jnp.int32, (c, c), 1)
    causal = col_i <= row_i
    g = g_ref[...]
    items = [(n, hh) for n in range(rows // c) for hh in range(heads)]

    def prep(n, hh):
        sl = slice(n * c, (n + 1) * c)
        lanes = slice(hh * GLA_DK, (hh + 1) * GLA_DK)
        b = b_all[sl, lanes]
        b_mid = b[c // 2 - 1:c // 2, :]
        b_end = b[c - 1:c, :]
        q = q_ref[sl, lanes].astype(F32)
        k = k_ref[sl, lanes].astype(F32)
        return dict(
            sl=sl,
            q_mid=(q * jnp.exp(b - b_mid)).astype(BF16),
            k_mid=(k * jnp.exp(b_mid - b)).astype(BF16),
            q_in=(q * jnp.exp(b)).astype(BF16),
            k_end=(k * jnp.exp(b_end - b)).astype(BF16),
            decay=jnp.exp(b_end))

    state = [st_ref[hh] for hh in range(heads)]
    cur = prep(*items[0])
    cur_s = _dot_nt(cur["q_mid"], cur["k_mid"])
    for i, (n, hh) in enumerate(items):
        if i + 1 < len(items):
            nxt = prep(*items[i + 1])
            nxt_s = _dot_nt(nxt["q_mid"], nxt["k_mid"])
        sl = cur["sl"]
        vcol = slice(hh * GLA_DV, (hh + 1) * GLA_DV)
        att = jnp.where(causal, cur_s, 0.0).astype(BF16)
        o = _dot(att, v_ref[sl, vcol]) + _dot_nt(cur["q_in"], state[hh].astype(BF16))
        state[hh] = state[hh] * cur["decay"] + _dot(vt_ref[0, hh, :, sl], cur["k_end"])
        o_ref[sl, vcol] = _rmsnorm(o, g).astype(BF16)
        if i + 1 < len(items):
            cur, cur_s = nxt, nxt_s
    for hh in range(heads):
        st_ref[hh] = state[hh]


def _gla(gq, gk, gv, gvt, la, g, tri, batch, seq, rows, heads):
    t = gq.shape[0]
    nr = seq // rows
    grid = (batch, GLA_HEADS // heads, nr)
    blk = lambda w: pl.BlockSpec((rows, heads * w), lambda b, h, r: (b * nr + r, h))
    return pl.pallas_call(
        functools.partial(_gla_kernel, rows=rows, heads=heads),
        grid=grid,
        in_specs=[blk(GLA_DK), blk(GLA_DK), blk(GLA_DV),
                  pl.BlockSpec((1, heads, GLA_DV, rows), lambda b, h, r: (b, h, 0, r)),
                  blk(GLA_DK), _const_spec(g.shape), _const_spec(tri.shape)],
        out_specs=blk(GLA_DV),
        out_shape=jax.ShapeDtypeStruct((t, GLA_HEADS * GLA_DV), BF16),
        scratch_shapes=[pltpu.VMEM((heads, GLA_DV, GLA_DK), F32)],
        compiler_params=pltpu.CompilerParams(
            dimension_semantics=("arbitrary", "arbitrary", "arbitrary"),
            vmem_limit_bytes=VMEM_LIMIT),
        name="gla",
    )(gq, gk, gv, gvt, la, g, tri)


def _mla_kernel(q_ref, k_ref, vt_ref, o_ref, *, heads, nq):
    tile = MLA_TILE

    def scores(qi, kb, hh):
        return _dot_nt(k_ref[0, hh, kb * tile:(kb + 1) * tile, :],
                       q_ref[0, hh, qi * tile:(qi + 1) * tile, :])

    def update(kb, hh, s, carry, masked):
        m, l, acc = carry
        if masked:
            key_i = lax.broadcasted_iota(jnp.int32, s.shape, 0)
            qry_i = lax.broadcasted_iota(jnp.int32, s.shape, 1)
            s = jnp.where(key_i <= qry_i, s, NEG_BIG)
        m_new = jnp.maximum(m, jnp.max(s, axis=0, keepdims=True))
        p = jnp.exp2(s - m_new)
        alpha = jnp.exp2(m - m_new)
        l = alpha * l + jnp.sum(p, axis=0, keepdims=True)
        acc = alpha * acc + _dot(vt_ref[0, hh, kb], p.astype(BF16))
        return m_new, l, acc

    init = (jnp.full((1, tile), NEG_BIG, F32), jnp.zeros((1, tile), F32), jnp.zeros((MLA_V, tile), F32))
    items = [(qi, kb, hh) for qi in range(nq) for kb in range(qi + 1) for hh in range(heads)]
    carry = {}
    s_cur = scores(*items[0])
    for idx, (qi, kb, hh) in enumerate(items):
        s_next = scores(*items[idx + 1]) if idx + 1 < len(items) else None
        diagonal = kb == qi
        c = update(kb, hh, s_cur, init if kb == 0 else carry[hh], diagonal)
        carry[hh] = c
        if diagonal:
            _, l, acc = c
            o_ref[qi * tile:(qi + 1) * tile, hh * MLA_V:(hh + 1) * MLA_V] = (acc / l).T.astype(BF16)
        s_cur = s_next


def _mla(mq, mk, mvt, batch, seq, heads):
    nq = seq // MLA_TILE
    grid = (batch, MLA_HEADS // heads)
    return pl.pallas_call(
        functools.partial(_mla_kernel, heads=heads, nq=nq),
        grid=grid,
        in_specs=[pl.BlockSpec((1, heads, seq, MLA_QK_PAD), lambda b, g: (b, g, 0, 0)),
                  pl.BlockSpec((1, heads, seq, MLA_QK_PAD), lambda b, g: (b, g, 0, 0)),
                  pl.BlockSpec((1, heads, nq, MLA_V, MLA_TILE), lambda b, g: (b, g, 0, 0, 0))],
        out_specs=pl.BlockSpec((seq, heads * MLA_V), lambda b, g: (b, g)),
        out_shape=jax.ShapeDtypeStruct((batch * seq, MLA_HEADS * MLA_V), BF16),
        compiler_params=pltpu.CompilerParams(
            dimension_semantics=("arbitrary", "arbitrary"), vmem_limit_bytes=VMEM_LIMIT),
        name="mla",
    )(mq, mk, mvt)


def _merge_kernel(x_ref, on_ref, om_ref, wg_ref, wgo_ref, wmo_ref, wout_ref, g_ref, b_ref, h_ref, *, sub):
    for r0 in range(0, x_ref.shape[0], sub):
        rows = slice(r0, r0 + sub)
        x = x_ref[rows, :]
        xb = x.astype(BF16)
        og = _dot(xb, wg_ref[:, 0:D_MODEL])
        gated = (on_ref[rows, :].astype(F32) * (og * _sigmoid(og))).astype(BF16)
        y_gla = _dot(gated, wgo_ref[...])
        y_mla = _dot(om_ref[rows, :], wmo_ref[...])
        ga = _sigmoid(_dot(xb, wg_ref[:, D_MODEL:2 * D_MODEL]))
        gb = _sigmoid(_dot(xb, wg_ref[:, 2 * D_MODEL:3 * D_MODEL]))
        mixed = _dot((ga * y_gla + gb * y_mla).astype(BF16), wout_ref[...])
        h_ref[rows, :] = _layernorm(DN_ALPHA * x + mixed, g_ref[...], b_ref[...])


MERGE_SUB = 512


def _merge(x2, o_norm, o_mla, wg, wgo, wmo, wout, g, b, tm):
    t = x2.shape[0]
    row = pl.BlockSpec((tm, D_MODEL), lambda i: (i, 0))
    return pl.pallas_call(
        functools.partial(_merge_kernel, sub=MERGE_SUB),
        grid=(t // tm,),
        in_specs=[row, row, row, _const_spec(wg.shape), _const_spec(wgo.shape),
                  _const_spec(wmo.shape), _const_spec(wout.shape),
                  _const_spec(g.shape), _const_spec(b.shape)],
        out_specs=row,
        out_shape=jax.ShapeDtypeStruct((t, D_MODEL), F32),
        compiler_params=pltpu.CompilerParams(
            dimension_semantics=("arbitrary",), vmem_limit_bytes=VMEM_LIMIT),
        name="merge",
    )(x2, o_norm, o_mla, wg, wgo, wmo, wout, g, b)


FFN_CHUNK = 256
FFN_NCHUNK = D_FF // FFN_CHUNK
HALO = 16


def _ffn_kernel(h_ref, halo_ref, wu_ref, cw_ref, cb_ref, wd_ref,
                g_ref, b_ref, o_ref, xs_ref, f_ref, ug_ref, uv_ref, *, tm, tiles_per_seq):
    i = pl.program_id(0)
    h = h_ref[...]
    seq_start = (i % tiles_per_seq) == 0
    xs_ref[0:HALO, :] = jnp.where(seq_start, 0.0, halo_ref[...]).astype(BF16)
    xs_ref[HALO:HALO + tm, :] = h.astype(BF16)

    def up(j, slot):
        xs = xs_ref[...]
        ug_ref[slot] = _dot(xs, wu_ref[:, j * FFN_CHUNK:(j + 1) * FFN_CHUNK])
        uv_ref[slot] = _dot(xs, wu_ref[:, D_FF + j * FFN_CHUNK:D_FF + (j + 1) * FFN_CHUNK])

    def conv(u_ref, slot, cw, cb):
        return (cw[2:3, :] * u_ref[slot, pl.ds(HALO, tm), :]
                + cw[1:2, :] * u_ref[slot, pl.ds(HALO - 1, tm), :]
                + cw[0:1, :] * u_ref[slot, pl.ds(HALO - 2, tm), :] + cb)

    up(0, 0)
    for j in range(FFN_NCHUNK):
        slot = j % 2
        if j + 1 < FFN_NCHUNK:
            up(j + 1, 1 - slot)
        gcol = slice(j * FFN_CHUNK, (j + 1) * FFN_CHUNK)
        vcol = slice(D_FF + j * FFN_CHUNK, D_FF + (j + 1) * FFN_CHUNK)
        a = conv(ug_ref, slot, cw_ref[:, gcol], cb_ref[:, gcol])
        v = conv(uv_ref, slot, cw_ref[:, vcol], cb_ref[:, vcol])
        f_ref[:, j * FFN_CHUNK:(j + 1) * FFN_CHUNK] = (a * _sigmoid(a) * v).astype(BF16)
    o_ref[...] = _layernorm(DN_ALPHA * h + _dot(f_ref[...], wd_ref[...]), g_ref[...], b_ref[...])


def _ffn(h1, wu, cw, cb, wd, g, b, seq, tm):
    t = h1.shape[0]
    row = pl.BlockSpec((tm, D_MODEL), lambda i: (i, 0))
    halo = pl.BlockSpec((HALO, D_MODEL), lambda i: (jnp.maximum(i * (tm // HALO) - 1, 0), 0))
    return pl.pallas_call(
        functools.partial(_ffn_kernel, tm=tm, tiles_per_seq=seq // tm),
        grid=(t // tm,),
        in_specs=[row, halo, _const_spec(wu.shape), _const_spec(cw.shape), _const_spec(cb.shape),
                  _const_spec(wd.shape),
                  _const_spec(g.shape), _const_spec(b.shape)],
        out_specs=row,
        out_shape=jax.ShapeDtypeStruct((t, D_MODEL), F32),
        scratch_shapes=[pltpu.VMEM((HALO + tm, D_MODEL), BF16), pltpu.VMEM((tm, D_FF), BF16),
                        pltpu.VMEM((2, HALO + tm, FFN_CHUNK), F32),
                        pltpu.VMEM((2, HALO + tm, FFN_CHUNK), F32)],
        compiler_params=pltpu.CompilerParams(
            dimension_semantics=("arbitrary",), vmem_limit_bytes=VMEM_LIMIT),
        name="ffn",
    )(h1, h1, wu, cw, cb, wd, g, b)


def _pack_weights(w_in, gla_w_gate_up, mla_w_uq, mla_w_ukv):
    o = 0
    parts = {}
    for name, n in (("q", 512), ("k", 512), ("v", 1024), ("r", GLA_RANK), ("og", 1024),
                    ("cq", MLA_Q_RANK), ("ckv", MLA_KV_RANK), ("kr", MLA_ROPE),
                    ("ga", D_MODEL), ("gb", D_MODEL)):
        parts[name] = w_in[:, o:o + n]
        o += n
    kr = parts["kr"]
    kr4 = jnp.concatenate([kr, kr], axis=1)
    r_pad = jnp.pad(parts["r"], ((0, 0), (0, LANES - GLA_RANK)))
    w_all = jnp.concatenate([parts["q"], parts["k"], parts["v"], parts["cq"], parts["ckv"],
                             kr4, r_pad], axis=1).astype(BF16)
    w_gates = jnp.concatenate([parts["og"], parts["ga"], parts["gb"]], axis=1).astype(BF16)

    wuq = mla_w_uq.reshape(MLA_Q_RANK, MLA_HEADS, MLA_NOPE + MLA_ROPE)
    nope = wuq[:, :, :MLA_NOPE].reshape(MLA_Q_RANK, MLA_HEADS * MLA_NOPE)
    rp = wuq[:, :, MLA_NOPE:]
    rope4 = jnp.concatenate([rp, rp], axis=2).reshape(MLA_Q_RANK, MLA_HEADS * LANES)
    wuq_p = jnp.concatenate([nope, rope4], axis=1).astype(BF16)
    wgu = jnp.pad(gla_w_gate_up, ((0, LANES - GLA_RANK), (0, 0))).astype(BF16)
    return w_all, w_gates, wuq_p, mla_w_ukv.astype(BF16), wgu


def kernel(x, positions, w_in, gla_w_gate_up, gla_b_gate, gla_norm_g, w_gla_o, mla_q_norm_g, mla_w_uq,
           mla_kv_norm_g, mla_w_ukv, w_mla_o, w_out, ln1_g, ln1_b, w_up, conv_w, conv_b, w_down,
           ln2_g, ln2_b):
    batch, seq, d = x.shape
    t = batch * seq
    depth = w_in.shape[0]
    half = MLA_ROPE // 2
    inv_freq = ROPE_THETA ** (-jnp.arange(half, dtype=F32) / half)
    invf = jnp.tile(inv_freq, LANES // half).reshape(1, LANES)
    pos_b = jnp.broadcast_to(positions.astype(F32).reshape(t, 1), (t, LANES))
    ridx = jnp.arange(GLA_TRI)
    tri = ((ridx[:, None] // GLA_SUPER == ridx[None, :] // GLA_SUPER) &
           (ridx[None, :] <= ridx[:, None])).astype(BF16)

    h = x.reshape(t, d)
    for l in range(depth):
        w_all, w_gates, wuq, wukv, wgu = _pack_weights(w_in[l], gla_w_gate_up[l], mla_w_uq[l], mla_w_ukv[l])
        gq, gk, gv, gvt, la, mq, mk, mvt = _inproj(
            h, pos_b, invf, w_all, wuq, wukv, wgu, gla_b_gate[l].reshape(1, -1),
            mla_q_norm_g[l].reshape(1, -1), mla_kv_norm_g[l].reshape(1, -1), batch, seq, tm=512)
        o_norm = _gla(gq, gk, gv, gvt, la, gla_norm_g[l].reshape(1, -1), tri, batch, seq,
                      rows=GLA_ROWS, heads=GLA_HEADS_PER_STEP)
        o_mla = _mla(mq, mk, mvt, batch, seq, heads=MLA_HEADS_PER_STEP)
        h1 = _merge(h, o_norm, o_mla, w_gates, w_gla_o[l].astype(BF16), w_mla_o[l].astype(BF16),
                    w_out[l].astype(BF16), ln1_g[l].reshape(1, -1), ln1_b[l].reshape(1, -1), tm=2 * MERGE_SUB)

        h = _ffn(h1, w_up[l].astype(BF16), conv_w[l], conv_b[l].reshape(1, -1), w_down[l].astype(BF16),
                 ln2_g[l].reshape(1, -1), ln2_b[l].reshape(1, -1), seq, tm=512)
    return h.reshape(batch, seq, d)
```

```python
import functools

import jax
import jax.numpy as jnp
from jax import lax
from jax.experimental import pallas as pl
from jax.experimental.pallas import tpu as pltpu

F32 = jnp.float32
BF16 = jnp.bfloat16

D_MODEL = 1024
GLA_HEADS = 4
GLA_DK = 128
GLA_DV = 256
GLA_RANK = 16
GLA_TAU = 16.0
GLA_CHUNK = 64
MLA_HEADS = 8
MLA_Q_RANK = 384
MLA_KV_RANK = 256
MLA_NOPE = 128
MLA_ROPE = 64
MLA_V = 128
ROPE_THETA = 10000.0
D_FF = 2816
DN_ALPHA = 2.0 ** 0.25
LN_EPS = 1e-5
RMS_EPS = 1e-6

LANES = 128
MLA_QK_PAD = 256
MLA_TILE = 512
MLA_HEADS_PER_STEP = 4
NEG_BIG = -1e30
LOG2_E = 1.4426950408889634

C_Q, C_K, C_V = 0, 512, 1024
C_CQ = 2048
C_KRR = C_CQ + MLA_Q_RANK
C_CKV = C_KRR + LANES
C_END = C_CKV + MLA_KV_RANK
R_LANE = MLA_ROPE

VMEM_LIMIT = 56 * 1024 * 1024


def _sigmoid(x):
    return 1.0 / (1.0 + jnp.exp(-x))


def _dot(a, b):
    return jnp.dot(a, b, preferred_element_type=F32)


def _dot_nt(a, b):
    return lax.dot_general(a, b, (((1,), (1,)), ((), ())), preferred_element_type=F32)


def _rmsnorm(x, g):
    ms = jnp.mean(x * x, axis=-1, keepdims=True)
    return x * lax.rsqrt(ms + RMS_EPS) * g


def _layernorm(x, g, b):
    mu = jnp.mean(x, axis=-1, keepdims=True)
    xc = x - mu
    var = jnp.mean(xc * xc, axis=-1, keepdims=True)
    return xc * lax.rsqrt(var + LN_EPS) * g + b


def _const_spec(shape):
    nd = len(shape)
    return pl.BlockSpec(shape, lambda *_: (0,) * nd, pipeline_mode=pl.Buffered(1))


def _inproj_kernel(x_ref, pos_ref, invf_ref, w_ref, wuq_ref, wukv_ref, wgu_ref, bg_ref,
                   gqn_ref, gkvn_ref,
                   gq_ref, gk_ref, gv_ref, gvt_ref, la_ref, mq_ref, mk_ref, mvt_ref):
    xb = x_ref[...].astype(BF16)

    def proj(lo, hi):
        return _dot(xb, w_ref[:, lo:hi])

    gq_ref[...] = (proj(C_Q, C_K) * (GLA_DK ** -0.5)).astype(BF16)
    gk_ref[...] = proj(C_K, C_V).astype(BF16)
    gv = proj(C_V, C_CQ)
    gv_ref[...] = gv.astype(BF16)
    for h in range(GLA_HEADS):
        gvt_ref[0, h, :, :] = gv[:, h * GLA_DV:(h + 1) * GLA_DV].T.astype(BF16)

    cq_krr = proj(C_CQ, C_CKV)
    krr = cq_krr[:, MLA_Q_RANK:]
    logit = _dot(krr.astype(BF16), wgu_ref[...]) + bg_ref[...]
    log_sig = jnp.minimum(logit, 0.0) - jnp.log(1.0 + jnp.exp(-jnp.abs(logit)))
    la_ref[...] = log_sig / GLA_TAU

    ang = pos_ref[...] * invf_ref[...]
    lane = lax.broadcasted_iota(jnp.int32, ang.shape, 1)
    cos = jnp.cos(ang)
    sin = jnp.sin(ang)
    low = lane < MLA_ROPE
    first = (lane % MLA_ROPE) < MLA_ROPE // 2
    sin_s = jnp.where(first, -sin, sin)

    scale = (MLA_NOPE + MLA_ROPE) ** -0.5 * LOG2_E
    cq = _rmsnorm(cq_krr[:, :MLA_Q_RANK], gqn_ref[...]).astype(BF16)
    q = _dot(cq, wuq_ref[...])
    nope_w = MLA_HEADS * MLA_NOPE
    for h in range(MLA_HEADS):
        mq_ref[0, h, :, 0:LANES] = (q[:, h * LANES:(h + 1) * LANES] * scale).astype(BF16)
    for pair in range(MLA_HEADS // 2):
        blk = q[:, nope_w + pair * LANES: nope_w + (pair + 1) * LANES]
        partner = jnp.where(first, pltpu.roll(blk, LANES - MLA_ROPE // 2, 1),
                            pltpu.roll(blk, MLA_ROPE // 2, 1))
        rot = (blk * cos + partner * sin_s) * scale
        mq_ref[0, 2 * pair, :, LANES:2 * LANES] = jnp.where(low, rot, 0.0).astype(BF16)
        mq_ref[0, 2 * pair + 1, :, LANES:2 * LANES] = jnp.where(
            low, pltpu.roll(rot, MLA_ROPE, 1), 0.0).astype(BF16)

    krr_rot = krr * cos + pltpu.roll(krr, MLA_ROPE // 2, 1) * sin_s
    kr = jnp.where(low, krr_rot, 0.0).astype(BF16)

    ckv = _rmsnorm(proj(C_CKV, C_END), gkvn_ref[...]).astype(BF16)
    kv = _dot(ckv, wukv_ref[...])
    for h in range(MLA_HEADS):
        base = h * (MLA_NOPE + MLA_V)
        mk_ref[0, h, :, 0:LANES] = kv[:, base:base + MLA_NOPE].astype(BF16)
        mk_ref[0, h, :, LANES:2 * LANES] = kr
        mvt_ref[0, h, 0, :, :] = kv[:, base + MLA_NOPE:base + MLA_NOPE + MLA_V].T.astype(BF16)


def _inproj(x2, pos_b, invf, w_all, wuq, wukv, wgu, bg, gqn, gkvn, batch, seq, tm):
    t = x2.shape[0]
    nj = seq // tm
    grid = (t // tm,)
    row = lambda w: pl.BlockSpec((tm, w), lambda i: (i, 0))
    head = lambda w: pl.BlockSpec((1, MLA_HEADS, tm, w), lambda i: (i // nj, 0, i % nj, 0))
    per_tile = MLA_TILE // tm
    vt_spec = pl.BlockSpec((1, MLA_HEADS, 1, MLA_V, tm),
                           lambda i: (i // nj, 0, (i % nj) // per_tile, 0, (i % nj) % per_tile))
    gvt_spec = pl.BlockSpec((1, GLA_HEADS, GLA_DV, tm), lambda i: (i // nj, 0, 0, i % nj))
    out_shape = (
        jax.ShapeDtypeStruct((t, GLA_HEADS * GLA_DK), BF16),
        jax.ShapeDtypeStruct((t, GLA_HEADS * GLA_DK), BF16),
        jax.ShapeDtypeStruct((t, GLA_HEADS * GLA_DV), BF16),
        jax.ShapeDtypeStruct((batch, GLA_HEADS, GLA_DV, seq), BF16),
        jax.ShapeDtypeStruct((t, GLA_HEADS * GLA_DK), F32),
        jax.ShapeDtypeStruct((batch, MLA_HEADS, seq, MLA_QK_PAD), BF16),
        jax.ShapeDtypeStruct((batch, MLA_HEADS, seq, MLA_QK_PAD), BF16),
        jax.ShapeDtypeStruct((batch, MLA_HEADS, seq // MLA_TILE, MLA_V, MLA_TILE), BF16),
    )
    return pl.pallas_call(
        _inproj_kernel,
        grid=grid,
        in_specs=[row(D_MODEL), row(LANES), _const_spec(invf.shape), _const_spec(w_all.shape),
                  _const_spec(wuq.shape), _const_spec(wukv.shape), _const_spec(wgu.shape),
                  _const_spec(bg.shape), _const_spec(gqn.shape), _const_spec(gkvn.shape)],
        out_specs=(row(512), row(512), row(1024), gvt_spec, row(512),
                   head(MLA_QK_PAD), head(MLA_QK_PAD), vt_spec),
        out_shape=out_shape,
        compiler_params=pltpu.CompilerParams(
            dimension_semantics=("arbitrary",), vmem_limit_bytes=VMEM_LIMIT),
        name="inproj",
    )(x2, pos_b, invf, w_all, wuq, wukv, wgu, bg, gqn, gkvn)


GLA_ROWS = 1024
GLA_HEADS_PER_STEP = 4
GLA_SUPER = 2 * GLA_CHUNK
GLA_TRI = 2 * GLA_SUPER


def _split2_dot(tri, x):
    hi = x.astype(BF16)
    lo = (x - hi.astype(F32)).astype(BF16)
    return _dot(tri, hi) + _dot(tri, lo)


def _gla_kernel(q_ref, k_ref, v_ref, vt_ref, la_ref, g_ref, tri_ref, o_ref, st_ref, *, rows, heads):
    @pl.when(pl.program_id(2) == 0)
    def _():
        st_ref[...] = jnp.zeros_like(st_ref)

    c = GLA_SUPER
    tri = tri_ref[...]
    b_all = jnp.concatenate([_split2_dot(tri, la_ref[r:r + GLA_TRI, :])
                             for r in range(0, rows, GLA_TRI)], axis=0)
    row_i = lax.broadcasted_iota(jnp.int32, (c, c), 0)
    col_i = lax.broadcasted_iota(jnp.int32, (c, c), 1)
    causal = col_i <= row_i
    g = g_ref[...]
    items = [(n, hh) for n in range(rows // c) for hh in range(heads)]

    def prep(n, hh):
        sl = slice(n * c, (n + 1) * c)
        lanes = slice(hh * GLA_DK, (hh + 1) * GLA_DK)
        b = b_all[sl, lanes]
        b_mid = b[c // 2 - 1:c // 2, :]
        b_end = b[c - 1:c, :]
        q = q_ref[sl, lanes].astype(F32)
        k = k_ref[sl, lanes].astype(F32)
        return dict(
            sl=sl,
            q_mid=(q * jnp.exp(b - b_mid)).astype(BF16),
            k_mid=(k * jnp.exp(b_mid - b)).astype(BF16),
            q_in=(q * jnp.exp(b)).astype(BF16),
            k_end=(k * jnp.exp(b_end - b)).astype(BF16),
            decay=jnp.exp(b_end))

    state = [st_ref[hh] for hh in range(heads)]
    cur = prep(*items[0])
    cur_s = _dot_nt(cur["q_mid"], cur["k_mid"])
    for i, (n, hh) in enumerate(items):
        if i + 1 < len(items):
            nxt = prep(*items[i + 1])
            nxt_s = _dot_nt(nxt["q_mid"], nxt["k_mid"])
        sl = cur["sl"]
        vcol = slice(hh * GLA_DV, (hh + 1) * GLA_DV)
        att = jnp.where(causal, cur_s, 0.0).astype(BF16)
        o = _dot(att, v_ref[sl, vcol]) + _dot_nt(cur["q_in"], state[hh].astype(BF16))
        state[hh] = state[hh] * cur["decay"] + _dot(vt_ref[0, hh, :, sl], cur["k_end"])
        o_ref[sl, vcol] = _rmsnorm(o, g).astype(BF16)
        if i + 1 < len(items):
            cur, cur_s = nxt, nxt_s
    for hh in range(heads):
        st_ref[hh] = state[hh]


def _gla(gq, gk, gv, gvt, la, g, tri, batch, seq, rows, heads):
    t = gq.shape[0]
    nr = seq // rows
    grid = (batch, GLA_HEADS // heads, nr)
    blk = lambda w: pl.BlockSpec((rows, heads * w), lambda b, h, r: (b * nr + r, h))
    return pl.pallas_call(
        functools.partial(_gla_kernel, rows=rows, heads=heads),
        grid=grid,
        in_specs=[blk(GLA_DK), blk(GLA_DK), blk(GLA_DV),
                  pl.BlockSpec((1, heads, GLA_DV, rows), lambda b, h, r: (b, h, 0, r)),
                  blk(GLA_DK), _const_spec(g.shape), _const_spec(tri.shape)],
        out_specs=blk(GLA_DV),
        out_shape=jax.ShapeDtypeStruct((t, GLA_HEADS * GLA_DV), BF16),
        scratch_shapes=[pltpu.VMEM((heads, GLA_DV, GLA_DK), F32)],
        compiler_params=pltpu.CompilerParams(
            dimension_semantics=("arbitrary", "arbitrary", "arbitrary"),
            vmem_limit_bytes=VMEM_LIMIT),
        name="gla",
    )(gq, gk, gv, gvt, la, g, tri)


def _mla_kernel(q_ref, k_ref, vt_ref, o_ref, *, heads, nq):
    tile = MLA_TILE

    def scores(qi, kb, hh):
        return _dot_nt(k_ref[0, hh, kb * tile:(kb + 1) * tile, :],
                       q_ref[0, hh, qi * tile:(qi + 1) * tile, :])

    def update(kb, hh, s, carry, masked):
        m, l, acc = carry
        if masked:
            key_i = lax.broadcasted_iota(jnp.int32, s.shape, 0)
            qry_i = lax.broadcasted_iota(jnp.int32, s.shape, 1)
            s = jnp.where(key_i <= qry_i, s, NEG_BIG)
        m_new = jnp.maximum(m, jnp.max(s, axis=0, keepdims=True))
        p = jnp.exp2(s - m_new)
        alpha = jnp.exp2(m - m_new)
        l = alpha * l + jnp.sum(p, axis=0, keepdims=True)
        acc = alpha * acc + _dot(vt_ref[0, hh, kb], p.astype(BF16))
        return m_new, l, acc

    init = (jnp.full((1, tile), NEG_BIG, F32), jnp.zeros((1, tile), F32), jnp.zeros((MLA_V, tile), F32))
    items = [(qi, kb, hh) for qi in range(nq) for kb in range(qi + 1) for hh in range(heads)]
    carry = {}
    s_cur = scores(*items[0])
    for idx, (qi, kb, hh) in enumerate(items):
        s_next = scores(*items[idx + 1]) if idx + 1 < len(items) else None
        diagonal = kb == qi
        c = update(kb, hh, s_cur, init if kb == 0 else carry[hh], diagonal)
        carry[hh] = c
        if diagonal:
            _, l, acc = c
            o_ref[qi * tile:(qi + 1) * tile, hh * MLA_V:(hh + 1) * MLA_V] = (acc / l).T.astype(BF16)
        s_cur = s_next


def _mla(mq, mk, mvt, batch, seq, heads):
    nq = seq // MLA_TILE
    grid = (batch, MLA_HEADS // heads)
    return pl.pallas_call(
        functools.partial(_mla_kernel, heads=heads, nq=nq),
        grid=grid,
        in_specs=[pl.BlockSpec((1, heads, seq, MLA_QK_PAD), lambda b, g: (b, g, 0, 0)),
                  pl.BlockSpec((1, heads, seq, MLA_QK_PAD), lambda b, g: (b, g, 0, 0)),
                  pl.BlockSpec((1, heads, nq, MLA_V, MLA_TILE), lambda b, g: (b, g, 0, 0, 0))],
        out_specs=pl.BlockSpec((seq, heads * MLA_V), lambda b, g: (b, g)),
        out_shape=jax.ShapeDtypeStruct((batch * seq, MLA_HEADS * MLA_V), BF16),
        compiler_params=pltpu.CompilerParams(
            dimension_semantics=("arbitrary", "arbitrary"), vmem_limit_bytes=VMEM_LIMIT),
        name="mla",
    )(mq, mk, mvt)


def _merge_kernel(x_ref, on_ref, om_ref, wg_ref, wgo_ref, wmo_ref, wout_ref, g_ref, b_ref, h_ref, *, sub):
    for r0 in range(0, x_ref.shape[0], sub):
        rows = slice(r0, r0 + sub)
        x = x_ref[rows, :]
        xb = x.astype(BF16)
        og = _dot(xb, wg_ref[:, 0:D_MODEL])
        gated = (on_ref[rows, :].astype(F32) * (og * _sigmoid(og))).astype(BF16)
        y_gla = _dot(gated, wgo_ref[...])
        y_mla = _dot(om_ref[rows, :], wmo_ref[...])
        ga = _sigmoid(_dot(xb, wg_ref[:, D_MODEL:2 * D_MODEL]))
        gb = _sigmoid(_dot(xb, wg_ref[:, 2 * D_MODEL:3 * D_MODEL]))
        mixed = _dot((ga * y_gla + gb * y_mla).astype(BF16), wout_ref[...])
        h_ref[rows, :] = _layernorm(DN_ALPHA * x + mixed, g_ref[...], b_ref[...])


MERGE_SUB = 512


def _merge(x2, o_norm, o_mla, wg, wgo, wmo, wout, g, b, tm):
    t = x2.shape[0]
    row = pl.BlockSpec((tm, D_MODEL), lambda i: (i, 0))
    return pl.pallas_call(
        functools.partial(_merge_kernel, sub=MERGE_SUB),
        grid=(t // tm,),
        in_specs=[row, row, row, _const_spec(wg.shape), _const_spec(wgo.shape),
                  _const_spec(wmo.shape), _const_spec(wout.shape),
                  _const_spec(g.shape), _const_spec(b.shape)],
        out_specs=row,
        out_shape=jax.ShapeDtypeStruct((t, D_MODEL), F32),
        compiler_params=pltpu.CompilerParams(
            dimension_semantics=("arbitrary",), vmem_limit_bytes=VMEM_LIMIT),
        name="merge",
    )(x2, o_norm, o_mla, wg, wgo, wmo, wout, g, b)


FFN_CHUNK = 256
FFN_NCHUNK = D_FF // FFN_CHUNK
HALO = 16


def _ffn_kernel(h_ref, halo_ref, wu_ref, cw_ref, cb_ref, wd_ref,
                g_ref, b_ref, o_ref, xs_ref, f_ref, ug_ref, uv_ref, *, tm, tiles_per_seq):
    i = pl.program_id(0)
    h = h_ref[...]
    seq_start = (i % tiles_per_seq) == 0
    xs_ref[0:HALO, :] = jnp.where(seq_start, 0.0, halo_ref[...]).astype(BF16)
    xs_ref[HALO:HALO + tm, :] = h.astype(BF16)

    def up(j, slot):
        xs = xs_ref[...]
        ug_ref[slot] = _dot(xs, wu_ref[:, j * FFN_CHUNK:(j + 1) * FFN_CHUNK])
        uv_ref[slot] = _dot(xs, wu_ref[:, D_FF + j * FFN_CHUNK:D_FF + (j + 1) * FFN_CHUNK])

    def conv(u_ref, slot, cw, cb):
        return (cw[2:3, :] * u_ref[slot, pl.ds(HALO, tm), :]
                + cw[1:2, :] * u_ref[slot, pl.ds(HALO - 1, tm), :]
                + cw[0:1, :] * u_ref[slot, pl.ds(HALO - 2, tm), :] + cb)

    up(0, 0)
    for j in range(FFN_NCHUNK):
        slot = j % 2
        if j + 1 < FFN_NCHUNK:
            up(j + 1, 1 - slot)
        gcol = slice(j * FFN_CHUNK, (j + 1) * FFN_CHUNK)
        vcol = slice(D_FF + j * FFN_CHUNK, D_FF + (j + 1) * FFN_CHUNK)
        a = conv(ug_ref, slot, cw_ref[:, gcol], cb_ref[:, gcol])
        v = conv(uv_ref, slot, cw_ref[:, vcol], cb_ref[:, vcol])
        f_ref[:, j * FFN_CHUNK:(j + 1) * FFN_CHUNK] = (a * _sigmoid(a) * v).astype(BF16)
    o_ref[...] = _layernorm(DN_ALPHA * h + _dot(f_ref[...], wd_ref[...]), g_ref[...], b_ref[...])


def _ffn(h1, wu, cw, cb, wd, g, b, seq, tm):
    t = h1.shape[0]
    row = pl.BlockSpec((tm, D_MODEL), lambda i: (i, 0))
    halo = pl.BlockSpec((HALO, D_MODEL), lambda i: (jnp.maximum(i * (tm // HALO) - 1, 0), 0))
    return pl.pallas_call(
        functools.partial(_ffn_kernel, tm=tm, tiles_per_seq=seq // tm),
        grid=(t // tm,),
        in_specs=[row, halo, _const_spec(wu.shape), _const_spec(cw.shape), _const_spec(cb.shape),
                  _const_spec(wd.shape),
                  _const_spec(g.shape), _const_spec(b.shape)],
        out_specs=row,
        out_shape=jax.ShapeDtypeStruct((t, D_MODEL), F32),
        scratch_shapes=[pltpu.VMEM((HALO + tm, D_MODEL), BF16), pltpu.VMEM((tm, D_FF), BF16),
                        pltpu.VMEM((2, HALO + tm, FFN_CHUNK), F32),
                        pltpu.VMEM((2, HALO + tm, FFN_CHUNK), F32)],
        compiler_params=pltpu.CompilerParams(
            dimension_semantics=("arbitrary",), vmem_limit_bytes=VMEM_LIMIT),
        name="ffn",
    )(h1, h1, wu, cw, cb, wd, g, b)


def _pack_weights(w_in, gla_w_gate_up, mla_w_uq, mla_w_ukv):
    o = 0
    parts = {}
    for name, n in (("q", 512), ("k", 512), ("v", 1024), ("r", GLA_RANK), ("og", 1024),
                    ("cq", MLA_Q_RANK), ("ckv", MLA_KV_RANK), ("kr", MLA_ROPE),
                    ("ga", D_MODEL), ("gb", D_MODEL)):
        parts[name] = w_in[:, o:o + n]
        o += n
    kr = parts["kr"]
    half = MLA_ROPE // 2
    r_pad = jnp.pad(parts["r"], ((0, 0), (0, half - GLA_RANK)))
    krr = jnp.concatenate([kr, r_pad, kr[:, half:]], axis=1)
    w_all = jnp.concatenate([parts["q"], parts["k"], parts["v"], parts["cq"], krr,
                             parts["ckv"]], axis=1).astype(BF16)
    w_gates = jnp.concatenate([parts["og"], parts["ga"], parts["gb"]], axis=1).astype(BF16)

    wuq = mla_w_uq.reshape(MLA_Q_RANK, MLA_HEADS, MLA_NOPE + MLA_ROPE)
    nope = wuq[:, :, :MLA_NOPE].reshape(MLA_Q_RANK, MLA_HEADS * MLA_NOPE)
    rope = wuq[:, :, MLA_NOPE:].reshape(MLA_Q_RANK, MLA_HEADS * MLA_ROPE)
    wuq_p = jnp.concatenate([nope, rope], axis=1).astype(BF16)
    wgu = jnp.pad(gla_w_gate_up, ((R_LANE, LANES - R_LANE - GLA_RANK), (0, 0))).astype(BF16)
    return w_all, w_gates, wuq_p, mla_w_ukv.astype(BF16), wgu


def kernel(x, positions, w_in, gla_w_gate_up, gla_b_gate, gla_norm_g, w_gla_o, mla_q_norm_g, mla_w_uq,
           mla_kv_norm_g, mla_w_ukv, w_mla_o, w_out, ln1_g, ln1_b, w_up, conv_w, conv_b, w_down,
           ln2_g, ln2_b):
    batch, seq, d = x.shape
    t = batch * seq
    depth = w_in.shape[0]
    half = MLA_ROPE // 2
    inv_freq = ROPE_THETA ** (-jnp.arange(half, dtype=F32) / half)
    invf = jnp.tile(inv_freq, LANES // half).reshape(1, LANES)
    pos_b = jnp.broadcast_to(positions.astype(F32).reshape(t, 1), (t, LANES))
    ridx = jnp.arange(GLA_TRI)
    tri = ((ridx[:, None] // GLA_SUPER == ridx[None, :] // GLA_SUPER) &
           (ridx[None, :] <= ridx[:, None])).astype(BF16)

    h = x.reshape(t, d)
    for l in range(depth):
        w_all, w_gates, wuq, wukv, wgu = _pack_weights(w_in[l], gla_w_gate_up[l], mla_w_uq[l], mla_w_ukv[l])
        gq, gk, gv, gvt, la, mq, mk, mvt = _inproj(
            h, pos_b, invf, w_all, wuq, wukv, wgu, gla_b_gate[l].reshape(1, -1),
            mla_q_norm_g[l].reshape(1, -1), mla_kv_norm_g[l].reshape(1, -1), batch, seq, tm=512)
        o_norm = _gla(gq, gk, gv, gvt, la, gla_norm_g[l].reshape(1, -1), tri, batch, seq,
                      rows=GLA_ROWS, heads=GLA_HEADS_PER_STEP)
        o_mla = _mla(mq, mk, mvt, batch, seq, heads=MLA_HEADS_PER_STEP)
        h1 = _merge(h, o_norm, o_mla, w_gates, w_gla_o[l].astype(BF16), w_mla_o[l].astype(BF16),
                    w_out[l].astype(BF16), ln1_g[l].reshape(1, -1), ln1_b[l].reshape(1, -1), tm=2 * MERGE_SUB)

        h = _ffn(h1, w_up[l].astype(BF16), conv_w[l], conv_b[l].reshape(1, -1), w_down[l].astype(BF16),
                 ln2_g[l].reshape(1, -1), ln2_b[l].reshape(1, -1), seq, tm=512)
    return h.reshape(batch, seq, d)
```

```python
import functools

import jax
import jax.numpy as jnp
from jax import lax
from jax.experimental import pallas as pl
from jax.experimental.pallas import tpu as pltpu

F32 = jnp.float32
BF16 = jnp.bfloat16

D_MODEL = 1024
GLA_HEADS = 4
GLA_DK = 128
GLA_DV = 256
GLA_RANK = 16
GLA_TAU = 16.0
GLA_CHUNK = 64
MLA_HEADS = 8
MLA_Q_RANK = 384
MLA_KV_RANK = 256
MLA_NOPE = 128
MLA_ROPE = 64
MLA_V = 128
ROPE_THETA = 10000.0
D_FF = 2816
DN_ALPHA = 2.0 ** 0.25
LN_EPS = 1e-5
RMS_EPS = 1e-6

LANES = 128
MLA_QK_PAD = 256
MLA_TILE = 512
MLA_HEADS_PER_STEP = 4
MLA_LROWS = 16
NEG_BIG = -1e30
LOG2_E = 1.4426950408889634

C_Q, C_K, C_V = 0, 512, 1024
C_CQ = 2048
C_KRR = C_CQ + MLA_Q_RANK
C_CKV = C_KRR + LANES
C_END = C_CKV + MLA_KV_RANK
R_LANE = MLA_ROPE

VMEM_LIMIT = 56 * 1024 * 1024


def _sigmoid(x):
    return 1.0 / (1.0 + jnp.exp(-x))


def _dot(a, b):
    return jnp.dot(a, b, preferred_element_type=F32)


def _dot_nt(a, b):
    return lax.dot_general(a, b, (((1,), (1,)), ((), ())), preferred_element_type=F32)


def _rmsnorm(x, g):
    ms = jnp.mean(x * x, axis=-1, keepdims=True)
    return x * lax.rsqrt(ms + RMS_EPS) * g


def _layernorm(x, g, b):
    mu = jnp.mean(x, axis=-1, keepdims=True)
    xc = x - mu
    var = jnp.mean(xc * xc, axis=-1, keepdims=True)
    return xc * lax.rsqrt(var + LN_EPS) * g + b


def _const_spec(shape):
    nd = len(shape)
    return pl.BlockSpec(shape, lambda *_: (0,) * nd, pipeline_mode=pl.Buffered(1))


def _inproj_kernel(x_ref, pos_ref, invf_ref, w_ref, wuq_ref, wukv_ref, wgu_ref, bg_ref,
                   gqn_ref, gkvn_ref,
                   gq_ref, gk_ref, gv_ref, gvt_ref, la_ref, mq_ref, mk_ref, mvt_ref):
    xb = x_ref[...].astype(BF16)

    def proj(lo, hi):
        return _dot(xb, w_ref[:, lo:hi])

    cq_krr = proj(C_CQ, C_CKV)
    ckv_raw = proj(C_CKV, C_END)
    krr = cq_krr[:, MLA_Q_RANK:]
    cq = _rmsnorm(cq_krr[:, :MLA_Q_RANK], gqn_ref[...]).astype(BF16)
    ckv = _rmsnorm(ckv_raw, gkvn_ref[...]).astype(BF16)

    gq_ref[...] = (proj(C_Q, C_K) * (GLA_DK ** -0.5)).astype(BF16)
    gk_ref[...] = proj(C_K, C_V).astype(BF16)
    gv = proj(C_V, C_CQ)
    gv_ref[...] = gv.astype(BF16)
    for h in range(GLA_HEADS):
        gvt_ref[0, h, :, :] = gv[:, h * GLA_DV:(h + 1) * GLA_DV].T.astype(BF16)

    logit = _dot(krr.astype(BF16), wgu_ref[...]) + bg_ref[...]
    log_sig = jnp.minimum(logit, 0.0) - jnp.log(1.0 + jnp.exp(-jnp.abs(logit)))
    la_ref[...] = log_sig / GLA_TAU

    ang = pos_ref[...] * invf_ref[...]
    lane = lax.broadcasted_iota(jnp.int32, ang.shape, 1)
    cos = jnp.cos(ang)
    sin = jnp.sin(ang)
    low = lane < MLA_ROPE
    first = (lane % MLA_ROPE) < MLA_ROPE // 2
    sin_s = jnp.where(first, -sin, sin)

    scale = (MLA_NOPE + MLA_ROPE) ** -0.5 * LOG2_E
    q = _dot(cq, wuq_ref[...])
    nope_w = MLA_HEADS * MLA_NOPE
    for h in range(MLA_HEADS):
        mq_ref[0, h, :, 0:LANES] = (q[:, h * LANES:(h + 1) * LANES] * scale).astype(BF16)
    for pair in range(MLA_HEADS // 2):
        blk = q[:, nope_w + pair * LANES: nope_w + (pair + 1) * LANES]
        partner = jnp.where(first, pltpu.roll(blk, LANES - MLA_ROPE // 2, 1),
                            pltpu.roll(blk, MLA_ROPE // 2, 1))
        rot = (blk * cos + partner * sin_s) * scale
        mq_ref[0, 2 * pair, :, LANES:2 * LANES] = jnp.where(low, rot, 0.0).astype(BF16)
        mq_ref[0, 2 * pair + 1, :, LANES:2 * LANES] = jnp.where(
            low, pltpu.roll(rot, MLA_ROPE, 1), 0.0).astype(BF16)

    krr_rot = krr * cos + pltpu.roll(krr, MLA_ROPE // 2, 1) * sin_s
    kr = jnp.where(low, krr_rot, 0.0).astype(BF16)

    kv = _dot(ckv, wukv_ref[...])
    for h in range(MLA_HEADS):
        base = h * (MLA_NOPE + MLA_V)
        mk_ref[0, h, :, 0:LANES] = kv[:, base:base + MLA_NOPE].astype(BF16)
        mk_ref[0, h, :, LANES:2 * LANES] = kr
        mvt_ref[0, h, 0, :, :] = kv[:, base + MLA_NOPE:base + MLA_NOPE + MLA_V].T.astype(BF16)


def _inproj(x2, pos_b, invf, w_all, wuq, wukv, wgu, bg, gqn, gkvn, batch, seq, tm):
    t = x2.shape[0]
    nj = seq // tm
    grid = (t // tm,)
    row = lambda w: pl.BlockSpec((tm, w), lambda i: (i, 0))
    head = lambda w: pl.BlockSpec((1, MLA_HEADS, tm, w), lambda i: (i // nj, 0, i % nj, 0))
    per_tile = MLA_TILE // tm
    vt_spec = pl.BlockSpec((1, MLA_HEADS, 1, MLA_V, tm),
                           lambda i: (i // nj, 0, (i % nj) // per_tile, 0, (i % nj) % per_tile))
    gvt_spec = pl.BlockSpec((1, GLA_HEADS, GLA_DV, tm), lambda i: (i // nj, 0, 0, i % nj))
    out_shape = (
        jax.ShapeDtypeStruct((t, GLA_HEADS * GLA_DK), BF16),
        jax.ShapeDtypeStruct((t, GLA_HEADS * GLA_DK), BF16),
        jax.ShapeDtypeStruct((t, GLA_HEADS * GLA_DV), BF16),
        jax.ShapeDtypeStruct((batch, GLA_HEADS, GLA_DV, seq), BF16),
        jax.ShapeDtypeStruct((t, GLA_HEADS * GLA_DK), F32),
        jax.ShapeDtypeStruct((batch, MLA_HEADS, seq, MLA_QK_PAD), BF16),
        jax.ShapeDtypeStruct((batch, MLA_HEADS, seq, MLA_QK_PAD), BF16),
        jax.ShapeDtypeStruct((batch, MLA_HEADS, seq // MLA_TILE, MLA_V, MLA_TILE), BF16),
    )
    return pl.pallas_call(
        _inproj_kernel,
        grid=grid,
        in_specs=[row(D_MODEL), row(LANES), _const_spec(invf.shape), _const_spec(w_all.shape),
                  _const_spec(wuq.shape), _const_spec(wukv.shape), _const_spec(wgu.shape),
                  _const_spec(bg.shape), _const_spec(gqn.shape), _const_spec(gkvn.shape)],
        out_specs=(row(512), row(512), row(1024), gvt_spec, row(512),
                   head(MLA_QK_PAD), head(MLA_QK_PAD), vt_spec),
        out_shape=out_shape,
        compiler_params=pltpu.CompilerParams(
            dimension_semantics=("arbitrary",), vmem_limit_bytes=VMEM_LIMIT),
        name="inproj",
    )(x2, pos_b, invf, w_all, wuq, wukv, wgu, bg, gqn, gkvn)


GLA_ROWS = 1024
GLA_HEADS_PER_STEP = 4
GLA_SUPER = 2 * GLA_CHUNK
GLA_TRI = 2 * GLA_SUPER


def _split2_dot(tri, x):
    hi = x.astype(BF16)
    lo = (x - hi.astype(F32)).astype(BF16)
    return _dot(tri, hi) + _dot(tri, lo)


def _gla_kernel(q_ref, k_ref, v_ref, vt_ref, la_ref, g_ref, tri_ref, o_ref, st_ref, *, rows, heads):
    @pl.when(pl.program_id(2) == 0)
    def _():
        st_ref[...] = jnp.zeros_like(st_ref)

    c = GLA_SUPER
    tri = tri_ref[...]
    b_all = jnp.concatenate([_split2_dot(tri, la_ref[r:r + GLA_TRI, :])
                             for r in range(0, rows, GLA_TRI)], axis=0)
    row_i = lax.broadcasted_iota(jnp.int32, (c, c), 0)
    col_i = lax.broadcasted_iota(jnp.int32, (c, c), 1)
    causal = col_i <= row_i
    g = g_ref[...]
    items = [(n, hh) for n in range(rows // c) for hh in range(heads)]

    def prep(n, hh):
        sl = slice(n * c, (n + 1) * c)
        lanes = slice(hh * GLA_DK, (hh + 1) * GLA_DK)
        b = b_all[sl, lanes]
        b_mid = b[c // 2 - 1:c // 2, :]
        b_end = b[c - 1:c, :]
        q = q_ref[sl, lanes].astype(F32)
        k = k_ref[sl, lanes].astype(F32)
        q_mid = q * jnp.exp(b - b_mid)
        k_mid = k * jnp.exp(b_mid - b)
        return dict(
            sl=sl,
            q_mid=q_mid.astype(BF16),
            k_mid=k_mid.astype(BF16),
            q_in=(q_mid * jnp.exp(b_mid)).astype(BF16),
            k_end=(k_mid * jnp.exp(b_end - b_mid)).astype(BF16),
            decay=jnp.exp(b_end))

    state = [st_ref[hh] for hh in range(heads)]
    cur = prep(*items[0])
    cur_s = _dot_nt(cur["q_mid"], cur["k_mid"])
    for i, (n, hh) in enumerate(items):
        if i + 1 < len(items):
            nxt = prep(*items[i + 1])
            nxt_s = _dot_nt(nxt["q_mid"], nxt["k_mid"])
        sl = cur["sl"]
        vcol = slice(hh * GLA_DV, (hh + 1) * GLA_DV)
        att = jnp.where(causal, cur_s, 0.0).astype(BF16)
        o = _dot(att, v_ref[sl, vcol]) + _dot_nt(cur["q_in"], state[hh].astype(BF16))
        state[hh] = state[hh] * cur["decay"] + _dot(vt_ref[0, hh, :, sl], cur["k_end"])
        o_ref[sl, vcol] = _rmsnorm(o, g).astype(BF16)
        if i + 1 < len(items):
            cur, cur_s = nxt, nxt_s
    for hh in range(heads):
        st_ref[hh] = state[hh]


def _gla(gq, gk, gv, gvt, la, g, tri, batch, seq, rows, heads):
    t = gq.shape[0]
    nr = seq // rows
    grid = (batch, GLA_HEADS // heads, nr)
    blk = lambda w: pl.BlockSpec((rows, heads * w), lambda b, h, r: (b * nr + r, h))
    return pl.pallas_call(
        functools.partial(_gla_kernel, rows=rows, heads=heads),
        grid=grid,
        in_specs=[blk(GLA_DK), blk(GLA_DK), blk(GLA_DV),
                  pl.BlockSpec((1, heads, GLA_DV, rows), lambda b, h, r: (b, h, 0, r)),
                  blk(GLA_DK), _const_spec(g.shape), _const_spec(tri.shape)],
        out_specs=blk(GLA_DV),
        out_shape=jax.ShapeDtypeStruct((t, GLA_HEADS * GLA_DV), BF16),
        scratch_shapes=[pltpu.VMEM((heads, GLA_DV, GLA_DK), F32)],
        compiler_params=pltpu.CompilerParams(
            dimension_semantics=("arbitrary", "arbitrary", "arbitrary"),
            vmem_limit_bytes=VMEM_LIMIT),
        name="gla",
    )(gq, gk, gv, gvt, la, g, tri)


def _mla_kernel(q_ref, k_ref, vt_ref, o_ref, *, heads, nq):
    tile = MLA_TILE

    def scores(qi, kb, hh):
        return _dot_nt(k_ref[0, hh, kb * tile:(kb + 1) * tile, :],
                       q_ref[0, hh, qi * tile:(qi + 1) * tile, :])

    ones = jnp.ones((MLA_LROWS, tile), BF16)

    def update(kb, hh, s, carry, masked):
        m, acc = carry
        if masked:
            key_i = lax.broadcasted_iota(jnp.int32, s.shape, 0)
            qry_i = lax.broadcasted_iota(jnp.int32, s.shape, 1)
            s = jnp.where(key_i <= qry_i, s, NEG_BIG)
        m_new = jnp.maximum(m, jnp.max(s, axis=0, keepdims=True))
        p = jnp.exp2(s - m_new)
        alpha = jnp.exp2(m - m_new)
        vt1 = jnp.concatenate([vt_ref[0, hh, kb], ones], axis=0)
        acc = alpha * acc + _dot(vt1, p.astype(BF16))
        return m_new, acc

    init = (jnp.full((1, tile), NEG_BIG, F32), jnp.zeros((MLA_V + MLA_LROWS, tile), F32))
    items = [(qi, kb, hh) for qi in range(nq) for kb in range(qi + 1) for hh in range(heads)]
    carry = {}
    s_cur = scores(*items[0])
    for idx, (qi, kb, hh) in enumerate(items):
        s_next = scores(*items[idx + 1]) if idx + 1 < len(items) else None
        diagonal = kb == qi
        c = update(kb, hh, s_cur, init if kb == 0 else carry[hh], diagonal)
        carry[hh] = c
        if diagonal:
            acc = c[1]
            out = acc[:MLA_V, :] / acc[MLA_V:MLA_V + 1, :]
            o_ref[qi * tile:(qi + 1) * tile, hh * MLA_V:(hh + 1) * MLA_V] = out.T.astype(BF16)
        s_cur = s_next


def _mla(mq, mk, mvt, batch, seq, heads):
    nq = seq // MLA_TILE
    grid = (batch, MLA_HEADS // heads)
    return pl.pallas_call(
        functools.partial(_mla_kernel, heads=heads, nq=nq),
        grid=grid,
        in_specs=[pl.BlockSpec((1, heads, seq, MLA_QK_PAD), lambda b, g: (b, g, 0, 0)),
                  pl.BlockSpec((1, heads, seq, MLA_QK_PAD), lambda b, g: (b, g, 0, 0)),
                  pl.BlockSpec((1, heads, nq, MLA_V, MLA_TILE), lambda b, g: (b, g, 0, 0, 0))],
        out_specs=pl.BlockSpec((seq, heads * MLA_V), lambda b, g: (b, g)),
        out_shape=jax.ShapeDtypeStruct((batch * seq, MLA_HEADS * MLA_V), BF16),
        compiler_params=pltpu.CompilerParams(
            dimension_semantics=("arbitrary", "arbitrary"), vmem_limit_bytes=VMEM_LIMIT),
        name="mla",
    )(mq, mk, mvt)


def _merge_kernel(x_ref, on_ref, om_ref, wg_ref, wgo_ref, wmo_ref, wout_ref, g_ref, b_ref, h_ref, *, sub):
    for r0 in range(0, x_ref.shape[0], sub):
        rows = slice(r0, r0 + sub)
        x = x_ref[rows, :]
        xb = x.astype(BF16)
        og = _dot(xb, wg_ref[:, 0:D_MODEL])
        y_mla = _dot(om_ref[rows, :], wmo_ref[...])
        ga = _sigmoid(_dot(xb, wg_ref[:, D_MODEL:2 * D_MODEL]))
        gb = _sigmoid(_dot(xb, wg_ref[:, 2 * D_MODEL:3 * D_MODEL]))
        gated = (on_ref[rows, :].astype(F32) * (og * _sigmoid(og))).astype(BF16)
        y_gla = _dot(gated, wgo_ref[...])
        mixed = _dot((ga * y_gla + gb * y_mla).astype(BF16), wout_ref[...])
        h_ref[rows, :] = _layernorm(DN_ALPHA * x + mixed, g_ref[...], b_ref[...])


MERGE_SUB = 512


def _merge(x2, o_norm, o_mla, wg, wgo, wmo, wout, g, b, tm):
    t = x2.shape[0]
    row = pl.BlockSpec((tm, D_MODEL), lambda i: (i, 0))
    return pl.pallas_call(
        functools.partial(_merge_kernel, sub=MERGE_SUB),
        grid=(t // tm,),
        in_specs=[row, row, row, _const_spec(wg.shape), _const_spec(wgo.shape),
                  _const_spec(wmo.shape), _const_spec(wout.shape),
                  _const_spec(g.shape), _const_spec(b.shape)],
        out_specs=row,
        out_shape=jax.ShapeDtypeStruct((t, D_MODEL), F32),
        compiler_params=pltpu.CompilerParams(
            dimension_semantics=("arbitrary",), vmem_limit_bytes=VMEM_LIMIT),
        name="merge",
    )(x2, o_norm, o_mla, wg, wgo, wmo, wout, g, b)


FFN_CHUNK = 256
FFN_NCHUNK = D_FF // FFN_CHUNK
FFN_SUB = 512
HALO = 16


def _ffn_kernel(h_ref, halo_ref, wu_ref, cw_ref, cb_ref, wd_ref,
                g_ref, b_ref, o_ref, xs_ref, f_ref, ug_ref, uv_ref, *, tm, sub, tiles_per_seq):
    i = pl.program_id(0)
    seq_start = (i % tiles_per_seq) == 0
    nsub = tm // sub

    def up(t, j, slot):
        xs = xs_ref[t]
        ug_ref[slot] = _dot(xs, wu_ref[:, j * FFN_CHUNK:(j + 1) * FFN_CHUNK])
        uv_ref[slot] = _dot(xs, wu_ref[:, D_FF + j * FFN_CHUNK:D_FF + (j + 1) * FFN_CHUNK])

    def conv(u_ref, slot, cw, cb):
        return (cw[2:3, :] * u_ref[slot, pl.ds(HALO, sub), :]
                + cw[1:2, :] * u_ref[slot, pl.ds(HALO - 1, sub), :]
                + cw[0:1, :] * u_ref[slot, pl.ds(HALO - 2, sub), :] + cb)

    for t in range(nsub):
        if t == 0:
            xs_ref[t, 0:HALO, :] = jnp.where(seq_start, 0.0, halo_ref[...]).astype(BF16)
        else:
            xs_ref[t, 0:HALO, :] = h_ref[t * sub - HALO:t * sub, :].astype(BF16)
        xs_ref[t, HALO:HALO + sub, :] = h_ref[t * sub:(t + 1) * sub, :].astype(BF16)

    slot = 0
    for t in range(nsub):
        rows = slice(t * sub, (t + 1) * sub)
        up(t, 0, slot)
        for j in range(FFN_NCHUNK):
            if j + 1 < FFN_NCHUNK:
                up(t, j + 1, 1 - slot)
            gcol = slice(j * FFN_CHUNK, (j + 1) * FFN_CHUNK)
            vcol = slice(D_FF + j * FFN_CHUNK, D_FF + (j + 1) * FFN_CHUNK)
            a = conv(ug_ref, slot, cw_ref[:, gcol], cb_ref[:, gcol])
            v = conv(uv_ref, slot, cw_ref[:, vcol], cb_ref[:, vcol])
            f_ref[t, :, gcol] = (a * _sigmoid(a) * v).astype(BF16)
            slot = 1 - slot
        o_ref[rows, :] = _layernorm(DN_ALPHA * h_ref[rows, :] + _dot(f_ref[t], wd_ref[...]),
                                    g_ref[...], b_ref[...])


def _ffn(h1, wu, cw, cb, wd, g, b, seq, tm):
    t = h1.shape[0]
    row = pl.BlockSpec((tm, D_MODEL), lambda i: (i, 0))
    halo = pl.BlockSpec((HALO, D_MODEL), lambda i: (jnp.maximum(i * (tm // HALO) - 1, 0), 0))
    return pl.pallas_call(
        functools.partial(_ffn_kernel, tm=tm, sub=FFN_SUB, tiles_per_seq=seq // tm),
        grid=(t // tm,),
        in_specs=[row, halo, _const_spec(wu.shape), _const_spec(cw.shape), _const_spec(cb.shape),
                  _const_spec(wd.shape),
                  _const_spec(g.shape), _const_spec(b.shape)],
        out_specs=row,
        out_shape=jax.ShapeDtypeStruct((t, D_MODEL), F32),
        scratch_shapes=[pltpu.VMEM((tm // FFN_SUB, HALO + FFN_SUB, D_MODEL), BF16),
                        pltpu.VMEM((tm // FFN_SUB, FFN_SUB, D_FF), BF16),
                        pltpu.VMEM((2, HALO + FFN_SUB, FFN_CHUNK), F32),
                        pltpu.VMEM((2, HALO + FFN_SUB, FFN_CHUNK), F32)],
        compiler_params=pltpu.CompilerParams(
            dimension_semantics=("arbitrary",), vmem_limit_bytes=VMEM_LIMIT),
        name="ffn",
    )(h1, h1, wu, cw, cb, wd, g, b)


def _pack_weights(w_in, gla_w_gate_up, mla_w_uq, mla_w_ukv):
    o = 0
    parts = {}
    for name, n in (("q", 512), ("k", 512), ("v", 1024), ("r", GLA_RANK), ("og", 1024),
                    ("cq", MLA_Q_RANK), ("ckv", MLA_KV_RANK), ("kr", MLA_ROPE),
                    ("ga", D_MODEL), ("gb", D_MODEL)):
        parts[name] = w_in[:, o:o + n]
        o += n
    kr = parts["kr"]
    half = MLA_ROPE // 2
    r_pad = jnp.pad(parts["r"], ((0, 0), (0, half - GLA_RANK)))
    krr = jnp.concatenate([kr, r_pad, kr[:, half:]], axis=1)
    w_all = jnp.concatenate([parts["q"], parts["k"], parts["v"], parts["cq"], krr,
                             parts["ckv"]], axis=1).astype(BF16)
    w_gates = jnp.concatenate([parts["og"], parts["ga"], parts["gb"]], axis=1).astype(BF16)

    wuq = mla_w_uq.reshape(MLA_Q_RANK, MLA_HEADS, MLA_NOPE + MLA_ROPE)
    nope = wuq[:, :, :MLA_NOPE].reshape(MLA_Q_RANK, MLA_HEADS * MLA_NOPE)
    rope = wuq[:, :, MLA_NOPE:].reshape(MLA_Q_RANK, MLA_HEADS * MLA_ROPE)
    wuq_p = jnp.concatenate([nope, rope], axis=1).astype(BF16)
    wgu = jnp.pad(gla_w_gate_up, ((R_LANE, LANES - R_LANE - GLA_RANK), (0, 0))).astype(BF16)
    return w_all, w_gates, wuq_p, mla_w_ukv.astype(BF16), wgu


def kernel(x, positions, w_in, gla_w_gate_up, gla_b_gate, gla_norm_g, w_gla_o, mla_q_norm_g, mla_w_uq,
           mla_kv_norm_g, mla_w_ukv, w_mla_o, w_out, ln1_g, ln1_b, w_up, conv_w, conv_b, w_down,
           ln2_g, ln2_b):
    batch, seq, d = x.shape
    t = batch * seq
    depth = w_in.shape[0]
    half = MLA_ROPE // 2
    inv_freq = ROPE_THETA ** (-jnp.arange(half, dtype=F32) / half)
    invf = jnp.tile(inv_freq, LANES // half).reshape(1, LANES)
    pos_b = jnp.broadcast_to(positions.astype(F32).reshape(t, 1), (t, LANES))
    ridx = jnp.arange(GLA_TRI)
    tri = ((ridx[:, None] // GLA_SUPER == ridx[None, :] // GLA_SUPER) &
           (ridx[None, :] <= ridx[:, None])).astype(BF16)

    h = x.reshape(t, d)
    for l in range(depth):
        w_all, w_gates, wuq, wukv, wgu = _pack_weights(w_in[l], gla_w_gate_up[l], mla_w_uq[l], mla_w_ukv[l])
        gq, gk, gv, gvt, la, mq, mk, mvt = _inproj(
            h, pos_b, invf, w_all, wuq, wukv, wgu, gla_b_gate[l].reshape(1, -1),
            mla_q_norm_g[l].reshape(1, -1), mla_kv_norm_g[l].reshape(1, -1), batch, seq, tm=512)
        o_norm = _gla(gq, gk, gv, gvt, la, gla_norm_g[l].reshape(1, -1), tri, batch, seq,
                      rows=GLA_ROWS, heads=GLA_HEADS_PER_STEP)
        o_mla = _mla(mq, mk, mvt, batch, seq, heads=MLA_HEADS_PER_STEP)
        h1 = _merge(h, o_norm, o_mla, w_gates, w_gla_o[l].astype(BF16), w_mla_o[l].astype(BF16),
                    w_out[l].astype(BF16), ln1_g[l].reshape(1, -1), ln1_b[l].reshape(1, -1), tm=2 * MERGE_SUB)

        h = _ffn(h1, w_up[l].astype(BF16), conv_w[l], conv_b[l].reshape(1, -1), w_down[l].astype(BF16),
                 ln2_g[l].reshape(1, -1), ln2_b[l].reshape(1, -1), seq, tm=2 * FFN_SUB)
    return h.reshape(batch, seq, d)
```

```python
import functools

import jax
import jax.numpy as jnp
from jax import lax
from jax.experimental import pallas as pl
from jax.experimental.pallas import tpu as pltpu

F32 = jnp.float32
BF16 = jnp.bfloat16

D_MODEL = 1024
GLA_HEADS = 4
GLA_DK = 128
GLA_DV = 256
GLA_RANK = 16
GLA_TAU = 16.0
GLA_CHUNK = 64
MLA_HEADS = 8
MLA_Q_RANK = 384
MLA_KV_RANK = 256
MLA_NOPE = 128
MLA_ROPE = 64
MLA_V = 128
ROPE_THETA = 10000.0
D_FF = 2816
DN_ALPHA = 2.0 ** 0.25
LN_EPS = 1e-5
RMS_EPS = 1e-6

LANES = 128
MLA_QK_PAD = 256
MLA_TILE = 512
MLA_HEADS_PER_STEP = 4
MLA_LROWS = 16
NEG_BIG = -1e30
LOG2_E = 1.4426950408889634

C_Q, C_K, C_V = 0, 512, 1024
C_CQ = 2048
C_KRR = C_CQ + MLA_Q_RANK
C_CKV = C_KRR + LANES
C_END = C_CKV + MLA_KV_RANK
R_LANE = MLA_ROPE

VMEM_LIMIT = 56 * 1024 * 1024


def _sigmoid(x):
    return 1.0 / (1.0 + jnp.exp(-x))


def _dot(a, b):
    return jnp.dot(a, b, preferred_element_type=F32)


def _dot_nt(a, b):
    return lax.dot_general(a, b, (((1,), (1,)), ((), ())), preferred_element_type=F32)


def _rmsnorm(x, g):
    ms = jnp.mean(x * x, axis=-1, keepdims=True)
    return x * lax.rsqrt(ms + RMS_EPS) * g


def _layernorm(x, g, b):
    mu = jnp.mean(x, axis=-1, keepdims=True)
    xc = x - mu
    var = jnp.mean(xc * xc, axis=-1, keepdims=True)
    return xc * lax.rsqrt(var + LN_EPS) * g + b


def _const_spec(shape):
    nd = len(shape)
    return pl.BlockSpec(shape, lambda *_: (0,) * nd, pipeline_mode=pl.Buffered(1))


def _inproj_kernel(x_ref, pos_ref, invf_ref, w_ref, wuq_ref, wukv_ref, wgu_ref, bg_ref,
                   gqn_ref, gkvn_ref,
                   gq_ref, gk_ref, gv_ref, gvt_ref, la_ref, mq_ref, mk_ref, mvt_ref):
    xb = x_ref[...].astype(BF16)

    def proj(lo, hi):
        return _dot(xb, w_ref[:, lo:hi])

    cq_krr = proj(C_CQ, C_CKV)
    ckv_raw = proj(C_CKV, C_END)
    krr = cq_krr[:, MLA_Q_RANK:]
    cq = _rmsnorm(cq_krr[:, :MLA_Q_RANK], gqn_ref[...]).astype(BF16)
    ckv = _rmsnorm(ckv_raw, gkvn_ref[...]).astype(BF16)

    gq_ref[...] = (proj(C_Q, C_K) * (GLA_DK ** -0.5)).astype(BF16)
    gk_ref[...] = proj(C_K, C_V).astype(BF16)
    gv = proj(C_V, C_CQ)
    gv_ref[...] = gv.astype(BF16)
    for h in range(GLA_HEADS):
        gvt_ref[0, h, :, :] = gv[:, h * GLA_DV:(h + 1) * GLA_DV].T.astype(BF16)

    logit = _dot(krr.astype(BF16), wgu_ref[...]) + bg_ref[...]
    log_sig = jnp.minimum(logit, 0.0) - jnp.log(1.0 + jnp.exp(-jnp.abs(logit)))
    la_ref[...] = log_sig / GLA_TAU

    ang = pos_ref[...] * invf_ref[...]
    lane = lax.broadcasted_iota(jnp.int32, ang.shape, 1)
    cos = jnp.cos(ang)
    sin = jnp.sin(ang)
    low = lane < MLA_ROPE
    first = (lane % MLA_ROPE) < MLA_ROPE // 2
    sin_s = jnp.where(first, -sin, sin)

    scale = (MLA_NOPE + MLA_ROPE) ** -0.5 * LOG2_E
    q = _dot(cq, wuq_ref[...])
    nope_w = MLA_HEADS * MLA_NOPE
    for h in range(MLA_HEADS):
        mq_ref[0, h, :, 0:LANES] = (q[:, h * LANES:(h + 1) * LANES] * scale).astype(BF16)
    for pair in range(MLA_HEADS // 2):
        blk = q[:, nope_w + pair * LANES: nope_w + (pair + 1) * LANES]
        partner = jnp.where(first, pltpu.roll(blk, LANES - MLA_ROPE // 2, 1),
                            pltpu.roll(blk, MLA_ROPE // 2, 1))
        rot = (blk * cos + partner * sin_s) * scale
        mq_ref[0, 2 * pair, :, LANES:2 * LANES] = jnp.where(low, rot, 0.0).astype(BF16)
        mq_ref[0, 2 * pair + 1, :, LANES:2 * LANES] = jnp.where(
            low, pltpu.roll(rot, MLA_ROPE, 1), 0.0).astype(BF16)

    krr_rot = krr * cos + pltpu.roll(krr, MLA_ROPE // 2, 1) * sin_s
    kr = jnp.where(low, krr_rot, 0.0).astype(BF16)

    kv = _dot(ckv, wukv_ref[...])
    for h in range(MLA_HEADS):
        base = h * (MLA_NOPE + MLA_V)
        mk_ref[0, h, :, 0:LANES] = kv[:, base:base + MLA_NOPE].astype(BF16)
        mk_ref[0, h, :, LANES:2 * LANES] = kr
        mvt_ref[0, h, 0, :, :] = kv[:, base + MLA_NOPE:base + MLA_NOPE + MLA_V].T.astype(BF16)


def _inproj(x2, pos_b, invf, w_all, wuq, wukv, wgu, bg, gqn, gkvn, batch, seq, tm):
    t = x2.shape[0]
    nj = seq // tm
    grid = (t // tm,)
    row = lambda w: pl.BlockSpec((tm, w), lambda i: (i, 0))
    head = lambda w: pl.BlockSpec((1, MLA_HEADS, tm, w), lambda i: (i // nj, 0, i % nj, 0))
    per_tile = MLA_TILE // tm
    vt_spec = pl.BlockSpec((1, MLA_HEADS, 1, MLA_V, tm),
                           lambda i: (i // nj, 0, (i % nj) // per_tile, 0, (i % nj) % per_tile))
    gvt_spec = pl.BlockSpec((1, GLA_HEADS, GLA_DV, tm), lambda i: (i // nj, 0, 0, i % nj))
    out_shape = (
        jax.ShapeDtypeStruct((t, GLA_HEADS * GLA_DK), BF16),
        jax.ShapeDtypeStruct((t, GLA_HEADS * GLA_DK), BF16),
        jax.ShapeDtypeStruct((t, GLA_HEADS * GLA_DV), BF16),
        jax.ShapeDtypeStruct((batch, GLA_HEADS, GLA_DV, seq), BF16),
        jax.ShapeDtypeStruct((t, GLA_HEADS * GLA_DK), F32),
        jax.ShapeDtypeStruct((batch, MLA_HEADS, seq, MLA_QK_PAD), BF16),
        jax.ShapeDtypeStruct((batch, MLA_HEADS, seq, MLA_QK_PAD), BF16),
        jax.ShapeDtypeStruct((batch, MLA_HEADS, seq // MLA_TILE, MLA_V, MLA_TILE), BF16),
    )
    return pl.pallas_call(
        _inproj_kernel,
        grid=grid,
        in_specs=[row(D_MODEL), row(LANES), _const_spec(invf.shape), _const_spec(w_all.shape),
                  _const_spec(wuq.shape), _const_spec(wukv.shape), _const_spec(wgu.shape),
                  _const_spec(bg.shape), _const_spec(gqn.shape), _const_spec(gkvn.shape)],
        out_specs=(row(512), row(512), row(1024), gvt_spec, row(512),
                   head(MLA_QK_PAD), head(MLA_QK_PAD), vt_spec),
        out_shape=out_shape,
        compiler_params=pltpu.CompilerParams(
            dimension_semantics=("arbitrary",), vmem_limit_bytes=VMEM_LIMIT),
        name="inproj",
    )(x2, pos_b, invf, w_all, wuq, wukv, wgu, bg, gqn, gkvn)


GLA_ROWS = 2048
GLA_HEADS_PER_STEP = 4
GLA_SUPER = 2 * GLA_CHUNK
GLA_TRI = 2 * GLA_SUPER


def _split2_dot(tri, x):
    hi = x.astype(BF16)
    lo = (x - hi.astype(F32)).astype(BF16)
    return _dot(tri, hi) + _dot(tri, lo)


def _gla_kernel(q_ref, k_ref, v_ref, vt_ref, la_ref, g_ref, tri_ref, o_ref, st_ref, *, rows, heads):
    @pl.when(pl.program_id(2) == 0)
    def _():
        st_ref[...] = jnp.zeros_like(st_ref)

    c = GLA_SUPER
    tri = tri_ref[...]
    b_all = jnp.concatenate([_split2_dot(tri, la_ref[r:r + GLA_TRI, :])
                             for r in range(0, rows, GLA_TRI)], axis=0)
    row_i = lax.broadcasted_iota(jnp.int32, (c, c), 0)
    col_i = lax.broadcasted_iota(jnp.int32, (c, c), 1)
    causal = col_i <= row_i
    g = g_ref[...]
    items = [(n, hh) for n in range(rows // c) for hh in range(heads)]

    def prep(n, hh):
        sl = slice(n * c, (n + 1) * c)
        lanes = slice(hh * GLA_DK, (hh + 1) * GLA_DK)
        b = b_all[sl, lanes]
        b_mid = b[c // 2 - 1:c // 2, :]
        b_end = b[c - 1:c, :]
        q = q_ref[sl, lanes].astype(F32)
        k = k_ref[sl, lanes].astype(F32)
        q_mid = q * jnp.exp(b - b_mid)
        k_mid = k * jnp.exp(b_mid - b)
        return dict(
            sl=sl,
            q_mid=q_mid.astype(BF16),
            k_mid=k_mid.astype(BF16),
            q_in=(q_mid * jnp.exp(b_mid)).astype(BF16),
            k_end=(k_mid * jnp.exp(b_end - b_mid)).astype(BF16),
            decay=jnp.exp(b_end))

    state = [st_ref[hh] for hh in range(heads)]
    cur = prep(*items[0])
    cur_s = _dot_nt(cur["q_mid"], cur["k_mid"])
    for i, (n, hh) in enumerate(items):
        if i + 1 < len(items):
            nxt = prep(*items[i + 1])
            nxt_s = _dot_nt(nxt["q_mid"], nxt["k_mid"])
        sl = cur["sl"]
        vcol = slice(hh * GLA_DV, (hh + 1) * GLA_DV)
        att = jnp.where(causal, cur_s, 0.0).astype(BF16)
        o = _dot(att, v_ref[sl, vcol]) + _dot_nt(cur["q_in"], state[hh].astype(BF16))
        state[hh] = state[hh] * cur["decay"] + _dot(vt_ref[0, hh, :, sl], cur["k_end"])
        o_ref[sl, vcol] = _rmsnorm(o, g).astype(BF16)
        if i + 1 < len(items):
            cur, cur_s = nxt, nxt_s
    for hh in range(heads):
        st_ref[hh] = state[hh]


def _gla(gq, gk, gv, gvt, la, g, tri, batch, seq, rows, heads):
    t = gq.shape[0]
    nr = seq // rows
    grid = (batch, GLA_HEADS // heads, nr)
    blk = lambda w: pl.BlockSpec((rows, heads * w), lambda b, h, r: (b * nr + r, h))
    return pl.pallas_call(
        functools.partial(_gla_kernel, rows=rows, heads=heads),
        grid=grid,
        in_specs=[blk(GLA_DK), blk(GLA_DK), blk(GLA_DV),
                  pl.BlockSpec((1, heads, GLA_DV, rows), lambda b, h, r: (b, h, 0, r)),
                  blk(GLA_DK), _const_spec(g.shape), _const_spec(tri.shape)],
        out_specs=blk(GLA_DV),
        out_shape=jax.ShapeDtypeStruct((t, GLA_HEADS * GLA_DV), BF16),
        scratch_shapes=[pltpu.VMEM((heads, GLA_DV, GLA_DK), F32)],
        compiler_params=pltpu.CompilerParams(
            dimension_semantics=("arbitrary", "arbitrary", "arbitrary"),
            vmem_limit_bytes=VMEM_LIMIT),
        name="gla",
    )(gq, gk, gv, gvt, la, g, tri)


def _mla_kernel(q_ref, k_ref, vt_ref, o_ref, *, heads, nq):
    tile = MLA_TILE

    def scores(qi, kb, hh):
        return _dot_nt(k_ref[0, hh, kb * tile:(kb + 1) * tile, :],
                       q_ref[0, hh, qi * tile:(qi + 1) * tile, :])

    ones = jnp.ones((MLA_LROWS, tile), BF16)

    def update(kb, hh, s, carry, masked):
        m, acc = carry
        if masked:
            key_i = lax.broadcasted_iota(jnp.int32, s.shape, 0)
            qry_i = lax.broadcasted_iota(jnp.int32, s.shape, 1)
            s = jnp.where(key_i <= qry_i, s, NEG_BIG)
        m_new = jnp.maximum(m, jnp.max(s, axis=0, keepdims=True))
        p = jnp.exp2(s - m_new)
        alpha = jnp.exp2(m - m_new)
        vt1 = jnp.concatenate([vt_ref[0, hh, kb], ones], axis=0)
        acc = alpha * acc + _dot(vt1, p.astype(BF16))
        return m_new, acc

    init = (jnp.full((1, tile), NEG_BIG, F32), jnp.zeros((MLA_V + MLA_LROWS, tile), F32))
    items = [(qi, kb, hh) for qi in range(nq) for kb in range(qi + 1) for hh in range(heads)]
    carry = {}
    s_cur = scores(*items[0])
    for idx, (qi, kb, hh) in enumerate(items):
        s_next = scores(*items[idx + 1]) if idx + 1 < len(items) else None
        diagonal = kb == qi
        c = update(kb, hh, s_cur, init if kb == 0 else carry[hh], diagonal)
        carry[hh] = c
        if diagonal:
            acc = c[1]
            out = acc[:MLA_V, :] / acc[MLA_V:MLA_V + 1, :]
            o_ref[qi * tile:(qi + 1) * tile, hh * MLA_V:(hh + 1) * MLA_V] = out.T.astype(BF16)
        s_cur = s_next


def _mla(mq, mk, mvt, batch, seq, heads):
    nq = seq // MLA_TILE
    grid = (batch, MLA_HEADS // heads)
    return pl.pallas_call(
        functools.partial(_mla_kernel, heads=heads, nq=nq),
        grid=grid,
        in_specs=[pl.BlockSpec((1, heads, seq, MLA_QK_PAD), lambda b, g: (b, g, 0, 0)),
                  pl.BlockSpec((1, heads, seq, MLA_QK_PAD), lambda b, g: (b, g, 0, 0)),
                  pl.BlockSpec((1, heads, nq, MLA_V, MLA_TILE), lambda b, g: (b, g, 0, 0, 0))],
        out_specs=pl.BlockSpec((seq, heads * MLA_V), lambda b, g: (b, g)),
        out_shape=jax.ShapeDtypeStruct((batch * seq, MLA_HEADS * MLA_V), BF16),
        compiler_params=pltpu.CompilerParams(
            dimension_semantics=("arbitrary", "arbitrary"), vmem_limit_bytes=VMEM_LIMIT),
        name="mla",
    )(mq, mk, mvt)


def _merge_kernel(x_ref, on_ref, om_ref, wg_ref, wgo_ref, wmo_ref, wout_ref, g_ref, b_ref, h_ref, *, sub):
    for r0 in range(0, x_ref.shape[0], sub):
        rows = slice(r0, r0 + sub)
        x = x_ref[rows, :]
        xb = x.astype(BF16)
        og = _dot(xb, wg_ref[:, 0:D_MODEL])
        y_mla = _dot(om_ref[rows, :], wmo_ref[...])
        ga = _sigmoid(_dot(xb, wg_ref[:, D_MODEL:2 * D_MODEL]))
        gb = _sigmoid(_dot(xb, wg_ref[:, 2 * D_MODEL:3 * D_MODEL]))
        gated = (on_ref[rows, :].astype(F32) * (og * _sigmoid(og))).astype(BF16)
        y_gla = _dot(gated, wgo_ref[...])
        mixed = _dot((ga * y_gla + gb * y_mla).astype(BF16), wout_ref[...])
        h_ref[rows, :] = _layernorm(DN_ALPHA * x + mixed, g_ref[...], b_ref[...])


MERGE_SUB = 512


def _merge(x2, o_norm, o_mla, wg, wgo, wmo, wout, g, b, tm):
    t = x2.shape[0]
    row = pl.BlockSpec((tm, D_MODEL), lambda i: (i, 0))
    return pl.pallas_call(
        functools.partial(_merge_kernel, sub=MERGE_SUB),
        grid=(t // tm,),
        in_specs=[row, row, row, _const_spec(wg.shape), _const_spec(wgo.shape),
                  _const_spec(wmo.shape), _const_spec(wout.shape),
                  _const_spec(g.shape), _const_spec(b.shape)],
        out_specs=row,
        out_shape=jax.ShapeDtypeStruct((t, D_MODEL), F32),
        compiler_params=pltpu.CompilerParams(
            dimension_semantics=("arbitrary",), vmem_limit_bytes=VMEM_LIMIT),
        name="merge",
    )(x2, o_norm, o_mla, wg, wgo, wmo, wout, g, b)


FFN_CHUNK = 256
FFN_NCHUNK = D_FF // FFN_CHUNK
FFN_SUB = 512
HALO = 16


def _ffn_kernel(h_ref, halo_ref, wu_ref, cw_ref, cb_ref, wd_ref,
                g_ref, b_ref, o_ref, xs_ref, f_ref, ug_ref, uv_ref, *, tm, sub, tiles_per_seq):
    i = pl.program_id(0)
    seq_start = (i % tiles_per_seq) == 0
    nsub = tm // sub

    def up(t, j, slot):
        xs = xs_ref[t]
        ug_ref[slot] = _dot(xs, wu_ref[:, j * FFN_CHUNK:(j + 1) * FFN_CHUNK])
        uv_ref[slot] = _dot(xs, wu_ref[:, D_FF + j * FFN_CHUNK:D_FF + (j + 1) * FFN_CHUNK])

    def conv(u_ref, slot, cw, cb):
        return (cw[2:3, :] * u_ref[slot, pl.ds(HALO, sub), :]
                + cw[1:2, :] * u_ref[slot, pl.ds(HALO - 1, sub), :]
                + cw[0:1, :] * u_ref[slot, pl.ds(HALO - 2, sub), :] + cb)

    for t in range(nsub):
        if t == 0:
            xs_ref[t, 0:HALO, :] = jnp.where(seq_start, 0.0, halo_ref[...]).astype(BF16)
        else:
            xs_ref[t, 0:HALO, :] = h_ref[t * sub - HALO:t * sub, :].astype(BF16)
        xs_ref[t, HALO:HALO + sub, :] = h_ref[t * sub:(t + 1) * sub, :].astype(BF16)

    slot = 0
    for t in range(nsub):
        rows = slice(t * sub, (t + 1) * sub)
        up(t, 0, slot)
        for j in range(FFN_NCHUNK):
            if j + 1 < FFN_NCHUNK:
                up(t, j + 1, 1 - slot)
            gcol = slice(j * FFN_CHUNK, (j + 1) * FFN_CHUNK)
            vcol = slice(D_FF + j * FFN_CHUNK, D_FF + (j + 1) * FFN_CHUNK)
            a = conv(ug_ref, slot, cw_ref[:, gcol], cb_ref[:, gcol])
            v = conv(uv_ref, slot, cw_ref[:, vcol], cb_ref[:, vcol])
            f_ref[t, :, gcol] = (a * _sigmoid(a) * v).astype(BF16)
            slot = 1 - slot
        o_ref[rows, :] = _layernorm(DN_ALPHA * h_ref[rows, :] + _dot(f_ref[t], wd_ref[...]),
                                    g_ref[...], b_ref[...])


def _ffn(h1, wu, cw, cb, wd, g, b, seq, tm):
    t = h1.shape[0]
    row = pl.BlockSpec((tm, D_MODEL), lambda i: (i, 0))
    halo = pl.BlockSpec((HALO, D_MODEL), lambda i: (jnp.maximum(i * (tm // HALO) - 1, 0), 0))
    return pl.pallas_call(
        functools.partial(_ffn_kernel, tm=tm, sub=FFN_SUB, tiles_per_seq=seq // tm),
        grid=(t // tm,),
        in_specs=[row, halo, _const_spec(wu.shape), _const_spec(cw.shape), _const_spec(cb.shape),
                  _const_spec(wd.shape),
                  _const_spec(g.shape), _const_spec(b.shape)],
        out_specs=row,
        out_shape=jax.ShapeDtypeStruct((t, D_MODEL), F32),
        scratch_shapes=[pltpu.VMEM((tm // FFN_SUB, HALO + FFN_SUB, D_MODEL), BF16),
                        pltpu.VMEM((tm // FFN_SUB, FFN_SUB, D_FF), BF16),
                        pltpu.VMEM((2, HALO + FFN_SUB, FFN_CHUNK), F32),
                        pltpu.VMEM((2, HALO + FFN_SUB, FFN_CHUNK), F32)],
        compiler_params=pltpu.CompilerParams(
            dimension_semantics=("arbitrary",), vmem_limit_bytes=VMEM_LIMIT),
        name="ffn",
    )(h1, h1, wu, cw, cb, wd, g, b)


def _pack_weights(w_in, gla_w_gate_up, mla_w_uq, mla_w_ukv):
    w_in = w_in.astype(BF16)
    o = 0
    parts = {}
    for name, n in (("q", 512), ("k", 512), ("v", 1024), ("r", GLA_RANK), ("og", 1024),
                    ("cq", MLA_Q_RANK), ("ckv", MLA_KV_RANK), ("kr", MLA_ROPE),
                    ("ga", D_MODEL), ("gb", D_MODEL)):
        parts[name] = w_in[:, o:o + n]
        o += n
    kr = parts["kr"]
    half = MLA_ROPE // 2
    r_pad = jnp.pad(parts["r"], ((0, 0), (0, half - GLA_RANK)))
    krr = jnp.concatenate([kr, r_pad, kr[:, half:]], axis=1)
    w_all = jnp.concatenate([parts["q"], parts["k"], parts["v"], parts["cq"], krr,
                             parts["ckv"]], axis=1).astype(BF16)
    w_gates = jnp.concatenate([parts["og"], parts["ga"], parts["gb"]], axis=1).astype(BF16)

    wuq = mla_w_uq.reshape(MLA_Q_RANK, MLA_HEADS, MLA_NOPE + MLA_ROPE)
    nope = wuq[:, :, :MLA_NOPE].reshape(MLA_Q_RANK, MLA_HEADS * MLA_NOPE)
    rope = wuq[:, :, MLA_NOPE:].reshape(MLA_Q_RANK, MLA_HEADS * MLA_ROPE)
    wuq_p = jnp.concatenate([nope, rope], axis=1).astype(BF16)
    wgu = jnp.pad(gla_w_gate_up, ((R_LANE, LANES - R_LANE - GLA_RANK), (0, 0))).astype(BF16)
    return w_all, w_gates, wuq_p, mla_w_ukv.astype(BF16), wgu


def kernel(x, positions, w_in, gla_w_gate_up, gla_b_gate, gla_norm_g, w_gla_o, mla_q_norm_g, mla_w_uq,
           mla_kv_norm_g, mla_w_ukv, w_mla_o, w_out, ln1_g, ln1_b, w_up, conv_w, conv_b, w_down,
           ln2_g, ln2_b):
    batch, seq, d = x.shape
    t = batch * seq
    depth = w_in.shape[0]
    half = MLA_ROPE // 2
    inv_freq = ROPE_THETA ** (-jnp.arange(half, dtype=F32) / half)
    invf = jnp.tile(inv_freq, LANES // half).reshape(1, LANES)
    pos_b = jnp.broadcast_to(positions.astype(F32).reshape(t, 1), (t, LANES))
    ridx = jnp.arange(GLA_TRI)
    tri = ((ridx[:, None] // GLA_SUPER == ridx[None, :] // GLA_SUPER) &
           (ridx[None, :] <= ridx[:, None])).astype(BF16)

    h = x.reshape(t, d)
    for l in range(depth):
        w_all, w_gates, wuq, wukv, wgu = _pack_weights(w_in[l], gla_w_gate_up[l], mla_w_uq[l], mla_w_ukv[l])
        gq, gk, gv, gvt, la, mq, mk, mvt = _inproj(
            h, pos_b, invf, w_all, wuq, wukv, wgu, gla_b_gate[l].reshape(1, -1),
            mla_q_norm_g[l].reshape(1, -1), mla_kv_norm_g[l].reshape(1, -1), batch, seq, tm=512)
        o_norm = _gla(gq, gk, gv, gvt, la, gla_norm_g[l].reshape(1, -1), tri, batch, seq,
                      rows=min(GLA_ROWS, seq), heads=GLA_HEADS_PER_STEP)
        o_mla = _mla(mq, mk, mvt, batch, seq, heads=MLA_HEADS_PER_STEP)
        h1 = _merge(h, o_norm, o_mla, w_gates, w_gla_o[l].astype(BF16), w_mla_o[l].astype(BF16),
                    w_out[l].astype(BF16), ln1_g[l].reshape(1, -1), ln1_b[l].reshape(1, -1), tm=2 * MERGE_SUB)

        h = _ffn(h1, w_up[l].astype(BF16), conv_w[l], conv_b[l].reshape(1, -1), w_down[l].astype(BF16),
                 ln2_g[l].reshape(1, -1), ln2_b[l].reshape(1, -1), seq, tm=2 * FFN_SUB)
    return h.reshape(batch, seq, d)
```

```python
import functools

import jax
import jax.numpy as jnp
from jax import lax
from jax.experimental import pallas as pl
from jax.experimental.pallas import tpu as pltpu

F32 = jnp.float32
BF16 = jnp.bfloat16

D_MODEL = 1024
GLA_HEADS = 4
GLA_DK = 128
GLA_DV = 256
GLA_RANK = 16
GLA_TAU = 16.0
GLA_CHUNK = 64
MLA_HEADS = 8
MLA_Q_RANK = 384
MLA_KV_RANK = 256
MLA_NOPE = 128
MLA_ROPE = 64
MLA_V = 128
ROPE_THETA = 10000.0
D_FF = 2816
DN_ALPHA = 2.0 ** 0.25
LN_EPS = 1e-5
RMS_EPS = 1e-6

LANES = 128
MLA_QK_PAD = 256
MLA_TILE = 512
MLA_HEADS_PER_STEP = 4
MLA_LROWS = 16
NEG_BIG = -1e30
LOG2_E = 1.4426950408889634

C_Q, C_K, C_V = 0, 512, 1024
C_CQ = 2048
C_KRR = C_CQ + MLA_Q_RANK
C_CKV = C_KRR + LANES
C_END = C_CKV + MLA_KV_RANK
R_LANE = MLA_ROPE

VMEM_LIMIT = 56 * 1024 * 1024


def _sigmoid(x):
    return 1.0 / (1.0 + jnp.exp(-x))


def _dot(a, b):
    return jnp.dot(a, b, preferred_element_type=F32)


def _dot_nt(a, b):
    return lax.dot_general(a, b, (((1,), (1,)), ((), ())), preferred_element_type=F32)


def _rmsnorm(x, g):
    ms = jnp.mean(x * x, axis=-1, keepdims=True)
    return x * lax.rsqrt(ms + RMS_EPS) * g


def _layernorm(x, g, b):
    mu = jnp.mean(x, axis=-1, keepdims=True)
    xc = x - mu
    var = jnp.mean(xc * xc, axis=-1, keepdims=True)
    return xc * lax.rsqrt(var + LN_EPS) * g + b


def _const_spec(shape):
    nd = len(shape)
    return pl.BlockSpec(shape, lambda *_: (0,) * nd, pipeline_mode=pl.Buffered(1))


def _inproj_kernel(x_ref, pos_ref, invf_ref, w_ref, wuq_ref, wukv_ref, wgu_ref, bg_ref,
                   gqn_ref, gkvn_ref,
                   gq_ref, gk_ref, gv_ref, gvt_ref, la_ref, mq_ref, mk_ref, mvt_ref):
    half_rows = x_ref.shape[0] // 2
    for r0 in (0, half_rows):
        rs = slice(r0, r0 + half_rows)
        xb = x_ref[rs, :].astype(BF16)

        def proj(lo, hi):
            return _dot(xb, w_ref[:, lo:hi])

        cq_krr = proj(C_CQ, C_CKV)
        ckv_raw = proj(C_CKV, C_END)
        krr = cq_krr[:, MLA_Q_RANK:]
        cq = _rmsnorm(cq_krr[:, :MLA_Q_RANK], gqn_ref[...]).astype(BF16)
        ckv = _rmsnorm(ckv_raw, gkvn_ref[...]).astype(BF16)

        gq_ref[rs, :] = (proj(C_Q, C_K) * (GLA_DK ** -0.5)).astype(BF16)
        gk_ref[rs, :] = proj(C_K, C_V).astype(BF16)
        gv = proj(C_V, C_CQ)
        gv_ref[rs, :] = gv.astype(BF16)
        for h in range(GLA_HEADS):
            gvt_ref[0, h, :, rs] = gv[:, h * GLA_DV:(h + 1) * GLA_DV].T.astype(BF16)

        logit = _dot(krr.astype(BF16), wgu_ref[...]) + bg_ref[...]
        log_sig = jnp.minimum(logit, 0.0) - jnp.log(1.0 + jnp.exp(-jnp.abs(logit)))
        la_ref[rs, :] = log_sig / GLA_TAU

        ang = pos_ref[rs, :] * invf_ref[...]
        lane = lax.broadcasted_iota(jnp.int32, ang.shape, 1)
        cos = jnp.cos(ang)
        sin = jnp.sin(ang)
        low = lane < MLA_ROPE
        first = (lane % MLA_ROPE) < MLA_ROPE // 2
        sin_s = jnp.where(first, -sin, sin)

        scale = (MLA_NOPE + MLA_ROPE) ** -0.5 * LOG2_E
        q = _dot(cq, wuq_ref[...])
        nope_w = MLA_HEADS * MLA_NOPE
        for h in range(MLA_HEADS):
            mq_ref[0, h, rs, 0:LANES] = (q[:, h * LANES:(h + 1) * LANES] * scale).astype(BF16)
        for pair in range(MLA_HEADS // 2):
            blk = q[:, nope_w + pair * LANES: nope_w + (pair + 1) * LANES]
            partner = jnp.where(first, pltpu.roll(blk, LANES - MLA_ROPE // 2, 1),
                                pltpu.roll(blk, MLA_ROPE // 2, 1))
            rot = (blk * cos + partner * sin_s) * scale
            mq_ref[0, 2 * pair, rs, LANES:2 * LANES] = jnp.where(low, rot, 0.0).astype(BF16)
            mq_ref[0, 2 * pair + 1, rs, LANES:2 * LANES] = jnp.where(
                low, pltpu.roll(rot, MLA_ROPE, 1), 0.0).astype(BF16)

        krr_rot = krr * cos + pltpu.roll(krr, MLA_ROPE // 2, 1) * sin_s
        kr = jnp.where(low, krr_rot, 0.0).astype(BF16)

        kv = _dot(ckv, wukv_ref[...])
        for h in range(MLA_HEADS):
            base = h * (MLA_NOPE + MLA_V)
            mk_ref[0, h, rs, 0:LANES] = kv[:, base:base + MLA_NOPE].astype(BF16)
            mk_ref[0, h, rs, LANES:2 * LANES] = kr
            mvt_ref[0, h, 0, :, rs] = kv[:, base + MLA_NOPE:base + MLA_NOPE + MLA_V].T.astype(BF16)


def _inproj(x2, pos_b, invf, w_all, wuq, wukv, wgu, bg, gqn, gkvn, batch, seq, tm):
    t = x2.shape[0]
    nj = seq // tm
    grid = (t // tm,)
    row = lambda w: pl.BlockSpec((tm, w), lambda i: (i, 0))
    head = lambda w: pl.BlockSpec((1, MLA_HEADS, tm, w), lambda i: (i // nj, 0, i % nj, 0))
    per_tile = MLA_TILE // tm
    vt_spec = pl.BlockSpec((1, MLA_HEADS, 1, MLA_V, tm),
                           lambda i: (i // nj, 0, (i % nj) // per_tile, 0, (i % nj) % per_tile))
    gvt_spec = pl.BlockSpec((1, GLA_HEADS, GLA_DV, tm), lambda i: (i // nj, 0, 0, i % nj))
    out_shape = (
        jax.ShapeDtypeStruct((t, GLA_HEADS * GLA_DK), BF16),
        jax.ShapeDtypeStruct((t, GLA_HEADS * GLA_DK), BF16),
        jax.ShapeDtypeStruct((t, GLA_HEADS * GLA_DV), BF16),
        jax.ShapeDtypeStruct((batch, GLA_HEADS, GLA_DV, seq), BF16),
        jax.ShapeDtypeStruct((t, GLA_HEADS * GLA_DK), F32),
        jax.ShapeDtypeStruct((batch, MLA_HEADS, seq, MLA_QK_PAD), BF16),
        jax.ShapeDtypeStruct((batch, MLA_HEADS, seq, MLA_QK_PAD), BF16),
        jax.ShapeDtypeStruct((batch, MLA_HEADS, seq // MLA_TILE, MLA_V, MLA_TILE), BF16),
    )
    return pl.pallas_call(
        _inproj_kernel,
        grid=grid,
        in_specs=[row(D_MODEL), row(LANES), _const_spec(invf.shape), _const_spec(w_all.shape),
                  _const_spec(wuq.shape), _const_spec(wukv.shape), _const_spec(wgu.shape),
                  _const_spec(bg.shape), _const_spec(gqn.shape), _const_spec(gkvn.shape)],
        out_specs=(row(512), row(512), row(1024), gvt_spec, row(512),
                   head(MLA_QK_PAD), head(MLA_QK_PAD), vt_spec),
        out_shape=out_shape,
        compiler_params=pltpu.CompilerParams(
            dimension_semantics=("arbitrary",), vmem_limit_bytes=VMEM_LIMIT),
        name="inproj",
    )(x2, pos_b, invf, w_all, wuq, wukv, wgu, bg, gqn, gkvn)


GLA_ROWS = 2048
GLA_HEADS_PER_STEP = 4
GLA_SUPER = 2 * GLA_CHUNK
GLA_TRI = 2 * GLA_SUPER


def _split2_dot(tri, x):
    hi = x.astype(BF16)
    lo = (x - hi.astype(F32)).astype(BF16)
    return _dot(tri, hi) + _dot(tri, lo)


def _gla_kernel(q_ref, k_ref, v_ref, vt_ref, la_ref, g_ref, tri_ref, o_ref, st_ref, *, rows, heads):
    @pl.when(pl.program_id(2) == 0)
    def _():
        st_ref[...] = jnp.zeros_like(st_ref)

    c = GLA_SUPER
    tri = tri_ref[...]
    b_all = jnp.concatenate([_split2_dot(tri, la_ref[r:r + GLA_TRI, :])
                             for r in range(0, rows, GLA_TRI)], axis=0)
    row_i = lax.broadcasted_iota(jnp.int32, (c, c), 0)
    col_i = lax.broadcasted_iota(jnp.int32, (c, c), 1)
    causal = col_i <= row_i
    g = g_ref[...]
    items = [(n, hh) for n in range(rows // c) for hh in range(heads)]

    def prep(n, hh):
        sl = slice(n * c, (n + 1) * c)
        lanes = slice(hh * GLA_DK, (hh + 1) * GLA_DK)
        b = b_all[sl, lanes]
        b_mid = b[c // 2 - 1:c // 2, :]
        b_end = b[c - 1:c, :]
        q = q_ref[sl, lanes].astype(F32)
        k = k_ref[sl, lanes].astype(F32)
        q_mid = q * jnp.exp(b - b_mid)
        k_mid = k * jnp.exp(b_mid - b)
        return dict(
            sl=sl,
            q_mid=q_mid.astype(BF16),
            k_mid=k_mid.astype(BF16),
            q_in=(q_mid * jnp.exp(b_mid)).astype(BF16),
            k_end=(k_mid * jnp.exp(b_end - b_mid)).astype(BF16),
            decay=jnp.exp(b_end))

    state = [st_ref[hh] for hh in range(heads)]
    cur = prep(*items[0])
    cur_s = _dot_nt(cur["q_mid"], cur["k_mid"])
    for i, (n, hh) in enumerate(items):
        if i + 1 < len(items):
            nxt = prep(*items[i + 1])
            nxt_s = _dot_nt(nxt["q_mid"], nxt["k_mid"])
        sl = cur["sl"]
        vcol = slice(hh * GLA_DV, (hh + 1) * GLA_DV)
        att = jnp.where(causal, cur_s, 0.0).astype(BF16)
        o = _dot(att, v_ref[sl, vcol]) + _dot_nt(cur["q_in"], state[hh].astype(BF16))
        state[hh] = state[hh] * cur["decay"] + _dot(vt_ref[0, hh, :, sl], cur["k_end"])
        o_ref[sl, vcol] = _rmsnorm(o, g).astype(BF16)
        if i + 1 < len(items):
            cur, cur_s = nxt, nxt_s
    for hh in range(heads):
        st_ref[hh] = state[hh]


def _gla(gq, gk, gv, gvt, la, g, tri, batch, seq, rows, heads):
    t = gq.shape[0]
    nr = seq // rows
    grid = (batch, GLA_HEADS // heads, nr)
    blk = lambda w: pl.BlockSpec((rows, heads * w), lambda b, h, r: (b * nr + r, h))
    return pl.pallas_call(
        functools.partial(_gla_kernel, rows=rows, heads=heads),
        grid=grid,
        in_specs=[blk(GLA_DK), blk(GLA_DK), blk(GLA_DV),
                  pl.BlockSpec((1, heads, GLA_DV, rows), lambda b, h, r: (b, h, 0, r)),
                  blk(GLA_DK), _const_spec(g.shape), _const_spec(tri.shape)],
        out_specs=blk(GLA_DV),
        out_shape=jax.ShapeDtypeStruct((t, GLA_HEADS * GLA_DV), BF16),
        scratch_shapes=[pltpu.VMEM((heads, GLA_DV, GLA_DK), F32)],
        compiler_params=pltpu.CompilerParams(
            dimension_semantics=("arbitrary", "arbitrary", "arbitrary"),
            vmem_limit_bytes=VMEM_LIMIT),
        name="gla",
    )(gq, gk, gv, gvt, la, g, tri)


def _mla_kernel(q_ref, k_ref, vt_ref, o_ref, *, heads, nq):
    tile = MLA_TILE

    def scores(qi, kb, hh):
        return _dot_nt(k_ref[0, hh, kb * tile:(kb + 1) * tile, :],
                       q_ref[0, hh, qi * tile:(qi + 1) * tile, :])

    ones = jnp.ones((MLA_LROWS, tile), BF16)

    def update(kb, hh, s, carry, masked):
        m, acc = carry
        if masked:
            key_i = lax.broadcasted_iota(jnp.int32, s.shape, 0)
            qry_i = lax.broadcasted_iota(jnp.int32, s.shape, 1)
            s = jnp.where(key_i <= qry_i, s, NEG_BIG)
        m_new = jnp.maximum(m, jnp.max(s, axis=0, keepdims=True))
        p = jnp.exp2(s - m_new)
        alpha = jnp.exp2(m - m_new)
        vt1 = jnp.concatenate([vt_ref[0, hh, kb], ones], axis=0)
        acc = alpha * acc + _dot(vt1, p.astype(BF16))
        return m_new, acc

    init = (jnp.full((1, tile), NEG_BIG, F32), jnp.zeros((MLA_V + MLA_LROWS, tile), F32))
    items = [(qi, kb, hh) for qi in range(nq) for kb in range(qi + 1) for hh in range(heads)]
    carry = {}
    s_cur = scores(*items[0])
    for idx, (qi, kb, hh) in enumerate(items):
        s_next = scores(*items[idx + 1]) if idx + 1 < len(items) else None
        diagonal = kb == qi
        c = update(kb, hh, s_cur, init if kb == 0 else carry[hh], diagonal)
        carry[hh] = c
        if diagonal:
            acc = c[1]
            out = acc[:MLA_V, :] / acc[MLA_V:MLA_V + 1, :]
            o_ref[qi * tile:(qi + 1) * tile, hh * MLA_V:(hh + 1) * MLA_V] = out.T.astype(BF16)
        s_cur = s_next


def _mla(mq, mk, mvt, batch, seq, heads):
    nq = seq // MLA_TILE
    grid = (batch, MLA_HEADS // heads)
    return pl.pallas_call(
        functools.partial(_mla_kernel, heads=heads, nq=nq),
        grid=grid,
        in_specs=[pl.BlockSpec((1, heads, seq, MLA_QK_PAD), lambda b, g: (b, g, 0, 0)),
                  pl.BlockSpec((1, heads, seq, MLA_QK_PAD), lambda b, g: (b, g, 0, 0)),
                  pl.BlockSpec((1, heads, nq, MLA_V, MLA_TILE), lambda b, g: (b, g, 0, 0, 0))],
        out_specs=pl.BlockSpec((seq, heads * MLA_V), lambda b, g: (b, g)),
        out_shape=jax.ShapeDtypeStruct((batch * seq, MLA_HEADS * MLA_V), BF16),
        compiler_params=pltpu.CompilerParams(
            dimension_semantics=("arbitrary", "arbitrary"), vmem_limit_bytes=VMEM_LIMIT),
        name="mla",
    )(mq, mk, mvt)


def _merge_kernel(x_ref, on_ref, om_ref, wg_ref, wgo_ref, wmo_ref, wout_ref, g_ref, b_ref, h_ref, *, sub):
    for r0 in range(0, x_ref.shape[0], sub):
        rows = slice(r0, r0 + sub)
        x = x_ref[rows, :]
        xb = x.astype(BF16)
        og = _dot(xb, wg_ref[:, 0:D_MODEL])
        y_mla = _dot(om_ref[rows, :], wmo_ref[...])
        ga = _sigmoid(_dot(xb, wg_ref[:, D_MODEL:2 * D_MODEL]))
        gb = _sigmoid(_dot(xb, wg_ref[:, 2 * D_MODEL:3 * D_MODEL]))
        gated = (on_ref[rows, :].astype(F32) * (og * _sigmoid(og))).astype(BF16)
        y_gla = _dot(gated, wgo_ref[...])
        mixed = _dot((ga * y_gla + gb * y_mla).astype(BF16), wout_ref[...])
        h_ref[rows, :] = _layernorm(DN_ALPHA * x + mixed, g_ref[...], b_ref[...])


MERGE_SUB = 512


def _merge(x2, o_norm, o_mla, wg, wgo, wmo, wout, g, b, tm):
    t = x2.shape[0]
    row = pl.BlockSpec((tm, D_MODEL), lambda i: (i, 0))
    return pl.pallas_call(
        functools.partial(_merge_kernel, sub=MERGE_SUB),
        grid=(t // tm,),
        in_specs=[row, row, row, _const_spec(wg.shape), _const_spec(wgo.shape),
                  _const_spec(wmo.shape), _const_spec(wout.shape),
                  _const_spec(g.shape), _const_spec(b.shape)],
        out_specs=row,
        out_shape=jax.ShapeDtypeStruct((t, D_MODEL), F32),
        compiler_params=pltpu.CompilerParams(
            dimension_semantics=("arbitrary",), vmem_limit_bytes=VMEM_LIMIT),
        name="merge",
    )(x2, o_norm, o_mla, wg, wgo, wmo, wout, g, b)


FFN_CHUNK = 256
FFN_NCHUNK = D_FF // FFN_CHUNK
FFN_SUB = 512
HALO = 16


def _ffn_kernel(h_ref, halo_ref, wu_ref, cw_ref, cb_ref, wd_ref,
                g_ref, b_ref, o_ref, xs_ref, f_ref, ug_ref, uv_ref, *, tm, sub, tiles_per_seq):
    i = pl.program_id(0)
    seq_start = (i % tiles_per_seq) == 0
    nsub = tm // sub

    def up(t, j, slot):
        xs = xs_ref[t]
        ug_ref[slot] = _dot(xs, wu_ref[:, j * FFN_CHUNK:(j + 1) * FFN_CHUNK])
        uv_ref[slot] = _dot(xs, wu_ref[:, D_FF + j * FFN_CHUNK:D_FF + (j + 1) * FFN_CHUNK])

    def conv(u_ref, slot, cw, cb):
        return (cw[2:3, :] * u_ref[slot, pl.ds(HALO, sub), :]
                + cw[1:2, :] * u_ref[slot, pl.ds(HALO - 1, sub), :]
                + cw[0:1, :] * u_ref[slot, pl.ds(HALO - 2, sub), :] + cb)

    for t in range(nsub):
        if t == 0:
            xs_ref[t, 0:HALO, :] = jnp.where(seq_start, 0.0, halo_ref[...]).astype(BF16)
        else:
            xs_ref[t, 0:HALO, :] = h_ref[t * sub - HALO:t * sub, :].astype(BF16)
        xs_ref[t, HALO:HALO + sub, :] = h_ref[t * sub:(t + 1) * sub, :].astype(BF16)

    slot = 0
    for t in range(nsub):
        rows = slice(t * sub, (t + 1) * sub)
        up(t, 0, slot)
        for j in range(FFN_NCHUNK):
            if j + 1 < FFN_NCHUNK:
                up(t, j + 1, 1 - slot)
            gcol = slice(j * FFN_CHUNK, (j + 1) * FFN_CHUNK)
            vcol = slice(D_FF + j * FFN_CHUNK, D_FF + (j + 1) * FFN_CHUNK)
            a = conv(ug_ref, slot, cw_ref[:, gcol], cb_ref[:, gcol])
            v = conv(uv_ref, slot, cw_ref[:, vcol], cb_ref[:, vcol])
            f_ref[t, :, gcol] = (a * _sigmoid(a) * v).astype(BF16)
            slot = 1 - slot
        o_ref[rows, :] = _layernorm(DN_ALPHA * h_ref[rows, :] + _dot(f_ref[t], wd_ref[...]),
                                    g_ref[...], b_ref[...])


def _ffn(h1, wu, cw, cb, wd, g, b, seq, tm):
    t = h1.shape[0]
    row = pl.BlockSpec((tm, D_MODEL), lambda i: (i, 0))
    halo = pl.BlockSpec((HALO, D_MODEL), lambda i: (jnp.maximum(i * (tm // HALO) - 1, 0), 0))
    return pl.pallas_call(
        functools.partial(_ffn_kernel, tm=tm, sub=FFN_SUB, tiles_per_seq=seq // tm),
        grid=(t // tm,),
        in_specs=[row, halo, _const_spec(wu.shape), _const_spec(cw.shape), _const_spec(cb.shape),
                  _const_spec(wd.shape),
                  _const_spec(g.shape), _const_spec(b.shape)],
        out_specs=row,
        out_shape=jax.ShapeDtypeStruct((t, D_MODEL), F32),
        scratch_shapes=[pltpu.VMEM((tm // FFN_SUB, HALO + FFN_SUB, D_MODEL), BF16),
                        pltpu.VMEM((tm // FFN_SUB, FFN_SUB, D_FF), BF16),
                        pltpu.VMEM((2, HALO + FFN_SUB, FFN_CHUNK), F32),
                        pltpu.VMEM((2, HALO + FFN_SUB, FFN_CHUNK), F32)],
        compiler_params=pltpu.CompilerParams(
            dimension_semantics=("arbitrary",), vmem_limit_bytes=VMEM_LIMIT),
        name="ffn",
    )(h1, h1, wu, cw, cb, wd, g, b)


def _pack_weights(w_in, gla_w_gate_up, mla_w_uq, mla_w_ukv):
    w_in = w_in.astype(BF16)
    o = 0
    parts = {}
    for name, n in (("q", 512), ("k", 512), ("v", 1024), ("r", GLA_RANK), ("og", 1024),
                    ("cq", MLA_Q_RANK), ("ckv", MLA_KV_RANK), ("kr", MLA_ROPE),
                    ("ga", D_MODEL), ("gb", D_MODEL)):
        parts[name] = w_in[:, o:o + n]
        o += n
    kr = parts["kr"]
    half = MLA_ROPE // 2
    r_pad = jnp.pad(parts["r"], ((0, 0), (0, half - GLA_RANK)))
    krr = jnp.concatenate([kr, r_pad, kr[:, half:]], axis=1)
    w_all = jnp.concatenate([parts["q"], parts["k"], parts["v"], parts["cq"], krr,
                             parts["ckv"]], axis=1).astype(BF16)
    w_gates = jnp.concatenate([parts["og"], parts["ga"], parts["gb"]], axis=1).astype(BF16)

    wuq = mla_w_uq.reshape(MLA_Q_RANK, MLA_HEADS, MLA_NOPE + MLA_ROPE)
    nope = wuq[:, :, :MLA_NOPE].reshape(MLA_Q_RANK, MLA_HEADS * MLA_NOPE)
    rope = wuq[:, :, MLA_NOPE:].reshape(MLA_Q_RANK, MLA_HEADS * MLA_ROPE)
    wuq_p = jnp.concatenate([nope, rope], axis=1).astype(BF16)
    wgu = jnp.pad(gla_w_gate_up, ((R_LANE, LANES - R_LANE - GLA_RANK), (0, 0))).astype(BF16)
    return w_all, w_gates, wuq_p, mla_w_ukv.astype(BF16), wgu


def kernel(x, positions, w_in, gla_w_gate_up, gla_b_gate, gla_norm_g, w_gla_o, mla_q_norm_g, mla_w_uq,
           mla_kv_norm_g, mla_w_ukv, w_mla_o, w_out, ln1_g, ln1_b, w_up, conv_w, conv_b, w_down,
           ln2_g, ln2_b):
    batch, seq, d = x.shape
    t = batch * seq
    depth = w_in.shape[0]
    half = MLA_ROPE // 2
    inv_freq = ROPE_THETA ** (-jnp.arange(half, dtype=F32) / half)
    invf = jnp.tile(inv_freq, LANES // half).reshape(1, LANES)
    pos_b = jnp.broadcast_to(positions.astype(F32).reshape(t, 1), (t, LANES))
    ridx = jnp.arange(GLA_TRI)
    tri = ((ridx[:, None] // GLA_SUPER == ridx[None, :] // GLA_SUPER) &
           (ridx[None, :] <= ridx[:, None])).astype(BF16)

    h = x.reshape(t, d)
    for l in range(depth):
        w_all, w_gates, wuq, wukv, wgu = _pack_weights(w_in[l], gla_w_gate_up[l], mla_w_uq[l], mla_w_ukv[l])
        gq, gk, gv, gvt, la, mq, mk, mvt = _inproj(
            h, pos_b, invf, w_all, wuq, wukv, wgu, gla_b_gate[l].reshape(1, -1),
            mla_q_norm_g[l].reshape(1, -1), mla_kv_norm_g[l].reshape(1, -1), batch, seq, tm=512)
        o_norm = _gla(gq, gk, gv, gvt, la, gla_norm_g[l].reshape(1, -1), tri, batch, seq,
                      rows=min(GLA_ROWS, seq), heads=GLA_HEADS_PER_STEP)
        o_mla = _mla(mq, mk, mvt, batch, seq, heads=MLA_HEADS_PER_STEP)
        h1 = _merge(h, o_norm, o_mla, w_gates, w_gla_o[l].astype(BF16), w_mla_o[l].astype(BF16),
                    w_out[l].astype(BF16), ln1_g[l].reshape(1, -1), ln1_b[l].reshape(1, -1), tm=2 * MERGE_SUB)

        h = _ffn(h1, w_up[l].astype(BF16), conv_w[l], conv_b[l].reshape(1, -1), w_down[l].astype(BF16),
                 ln2_g[l].reshape(1, -1), ln2_b[l].reshape(1, -1), seq, tm=2 * FFN_SUB)
    return h.reshape(batch, seq, d)
```
